```python
import math
import jax, jax.numpy as jnp
from jax import lax
import numpy as np

D_MODEL = 1024
BATCH = 2
SEQ = 8192
DEPTH = 1

D_MIX = D_MODEL
SSM_WIDTH = D_MIX // 2
SSM_GROUP_CH = 16
SSM_GROUPS = SSM_WIDTH // SSM_GROUP_CH
SSM_STATE = 64
HEAD_DIM = 64
ATTN_WIDTH = D_MIX - SSM_WIDTH
ATTN_HEADS = ATTN_WIDTH // HEAD_DIM
Q_BLOCK = 128
DT_MIN = 0.001
DT_MAX = 0.1
FORGET_BIAS_MEAN = 2.0
RMS_EPS = 1e-6
IN_PROJ_WIDTH = 2 * SSM_WIDTH + 4 * ATTN_WIDTH + ATTN_HEADS
SPLITS = (SSM_WIDTH, 2 * SSM_WIDTH, 2 * SSM_WIDTH + ATTN_WIDTH,
          2 * SSM_WIDTH + 2 * ATTN_WIDTH, 2 * SSM_WIDTH + 3 * ATTN_WIDTH,
          2 * SSM_WIDTH + 4 * ATTN_WIDTH)

kernel_name = "hymba_s5_fox_sandwich_layer"


def _rmsnorm(x, g):
    xf = x.astype(jnp.float32)
    y = xf * lax.rsqrt(jnp.mean(xf * xf, axis=-1, keepdims=True) + RMS_EPS)
    return (y * g.astype(jnp.float32)).astype(x.dtype)


def _s5_branch(u, a_re, a_im, log_dt, b_re, b_im, c_re, c_im, d_skip, w_glu, b_glu):
    f32 = jnp.float32
    bsz, seq, _ = u.shape
    uf = u.astype(f32)
    ug = uf.reshape(bsz, seq, SSM_GROUPS, SSM_GROUP_CH)
    ar, ai = a_re.astype(f32), a_im.astype(f32)
    dt = jnp.exp(log_dt.astype(f32))[:, None]
    mag = jnp.exp(dt * ar)
    abar_re, abar_im = mag * jnp.cos(dt * ai), mag * jnp.sin(dt * ai)
    den = ar * ar + ai * ai
    nr, ni = abar_re - 1.0, abar_im
    coef_re = (nr * ar + ni * ai) / den
    coef_im = (ni * ar - nr * ai) / den
    br, bi = b_re.astype(f32), b_im.astype(f32)
    bbar_re = coef_re[..., None] * br - coef_im[..., None] * bi
    bbar_im = coef_re[..., None] * bi + coef_im[..., None] * br
    bu_re = jnp.einsum('bsgh,gnh->bsgn', ug, bbar_re)
    bu_im = jnp.einsum('bsgh,gnh->bsgn', ug, bbar_im)
    a_re_t = jnp.broadcast_to(abar_re, bu_re.shape)
    a_im_t = jnp.broadcast_to(abar_im, bu_im.shape)

    def combine(e1, e2):
        a1r, a1i, b1r, b1i = e1
        a2r, a2i, b2r, b2i = e2
        return (a1r * a2r - a1i * a2i,
                a1r * a2i + a1i * a2r,
                a2r * b1r - a2i * b1i + b2r,
                a2r * b1i + a2i * b1r + b2i)

    _, _, h_re, h_im = lax.associative_scan(combine, (a_re_t, a_im_t, bu_re, bu_im), axis=1)
    y = (jnp.einsum('bsgn,ghn->bsgh', h_re, c_re.astype(f32))
         - jnp.einsum('bsgn,ghn->bsgh', h_im, c_im.astype(f32)))
    y = y.reshape(bsz, seq, SSM_WIDTH) + d_skip.astype(f32) * uf
    y = jax.nn.gelu(y)
    y = y * jax.nn.sigmoid(y @ w_glu.astype(f32) + b_glu.astype(f32))
    return y.astype(u.dtype)


def _forgetting_attention(q, k, v, log_f):
    f32 = jnp.float32
    bsz, seq, nh, dh = q.shape
    n_blocks = seq // Q_BLOCK
    scale = dh ** -0.5
    kf, vf = k.astype(f32), v.astype(f32)
    cum = jnp.cumsum(log_f, axis=1)
    cum_k = cum.transpose(0, 2, 1)
    qb = (q.astype(f32) * scale).reshape(bsz, n_blocks, Q_BLOCK, nh, dh).transpose(1, 0, 2, 3, 4)
    cb = cum.reshape(bsz, n_blocks, Q_BLOCK, nh).transpose(1, 0, 3, 2)
    key_pos = jnp.arange(seq)

    def one_block(args):
        blk, qi, ci = args
        q_pos = blk * Q_BLOCK + jnp.arange(Q_BLOCK)
        s = jnp.einsum('bqhd,bkhd->bhqk', qi, kf)
        s = s + ci[..., :, None] - cum_k[:, :, None, :]
        s = jnp.where(key_pos[None, :] <= q_pos[:, None], s, -jnp.inf)
        p = jax.nn.softmax(s, axis=-1)
        return jnp.einsum('bhqk,bkhd->bqhd', p, vf)

    out = lax.map(one_block, (jnp.arange(n_blocks), qb, cb))
    return out.transpose(1, 0, 2, 3, 4).reshape(bsz, seq, nh * dh).astype(q.dtype)


def _hybrid_layer(x, norm_pre_g, w_in, b_forget, ssm_a_re, ssm_a_im, ssm_log_dt,
                  ssm_b_re, ssm_b_im, ssm_c_re, ssm_c_im, ssm_d, w_glu, b_glu,
                  g_ssm, g_attn, w_out, norm_post_g):
    bsz, seq, _ = x.shape
    xn = _rmsnorm(x, norm_pre_g)
    proj = xn @ w_in
    u_ssm, z_ssm, q, k, v, z_attn, f_logit = jnp.split(proj, SPLITS, axis=-1)
    y_ssm = _s5_branch(u_ssm, ssm_a_re, ssm_a_im, ssm_log_dt, ssm_b_re, ssm_b_im,
                       ssm_c_re, ssm_c_im, ssm_d, w_glu, b_glu)
    y_ssm = _rmsnorm(y_ssm * jax.nn.silu(z_ssm), g_ssm)
    shp = (bsz, seq, ATTN_HEADS, HEAD_DIM)
    log_f = jax.nn.log_sigmoid((f_logit + b_forget).astype(jnp.float32))
    y_attn = _forgetting_attention(q.reshape(shp), k.reshape(shp), v.reshape(shp), log_f)
    y_attn = _rmsnorm(y_attn * jax.nn.silu(z_attn), g_attn)
    y = jnp.concatenate([y_ssm, y_attn], axis=-1) @ w_out
    return x + _rmsnorm(y, norm_post_g)


def setup_inputs(seed: int = 0) -> dict:
    key = jax.random.key(seed)
    ks = jax.random.split(key, 20)
    f32 = jnp.float32
    L, G, N, H = DEPTH, SSM_GROUPS, SSM_STATE, SSM_GROUP_CH
    nrm = jax.random.normal
    x = nrm(ks[0], (BATCH, SEQ, D_MODEL), f32)
    norm_pre_g = 1.0 + 0.02 * nrm(ks[1], (L, D_MODEL), f32)
    w_in = nrm(ks[2], (L, D_MODEL, IN_PROJ_WIDTH), f32) * D_MODEL ** -0.5
    b_forget = FORGET_BIAS_MEAN + 0.5 * nrm(ks[3], (L, ATTN_HEADS), f32)
    n_idx = jnp.arange(N, dtype=f32)
    ssm_a_re = -0.5 + 0.01 * nrm(ks[4], (L, G, N), f32)
    ssm_a_im = math.pi * n_idx[None, None, :] + 0.01 * nrm(ks[5], (L, G, N), f32)
    ssm_log_dt = jax.random.uniform(ks[6], (L, G), f32, minval=math.log(DT_MIN), maxval=math.log(DT_MAX))
    ssm_b_re = nrm(ks[7], (L, G, N, H), f32) * (2 * H) ** -0.5
    ssm_b_im = nrm(ks[8], (L, G, N, H), f32) * (2 * H) ** -0.5
    ssm_c_re = nrm(ks[9], (L, G, H, N), f32) * (2 * N) ** -0.5
    ssm_c_im = nrm(ks[10], (L, G, H, N), f32) * (2 * N) ** -0.5
    ssm_d = nrm(ks[11], (L, SSM_WIDTH), f32)
    w_glu = nrm(ks[12], (L, SSM_WIDTH, SSM_WIDTH), f32) * SSM_WIDTH ** -0.5
    b_glu = 0.02 * nrm(ks[13], (L, SSM_WIDTH), f32)
    g_ssm = 1.0 + 0.02 * nrm(ks[14], (L, SSM_WIDTH), f32)
    g_attn = 1.0 + 0.02 * nrm(ks[15], (L, ATTN_WIDTH), f32)
    w_out = nrm(ks[16], (L, D_MIX, D_MODEL), f32) * D_MIX ** -0.5
    norm_post_g = 1.0 + 0.02 * nrm(ks[17], (L, D_MODEL), f32)
    return {"x": x, "norm_pre_g": norm_pre_g, "w_in": w_in, "b_forget": b_forget,
            "ssm_a_re": ssm_a_re, "ssm_a_im": ssm_a_im, "ssm_log_dt": ssm_log_dt,
            "ssm_b_re": ssm_b_re, "ssm_b_im": ssm_b_im, "ssm_c_re": ssm_c_re,
            "ssm_c_im": ssm_c_im, "ssm_d": ssm_d, "w_glu": w_glu, "b_glu": b_glu,
            "g_ssm": g_ssm, "g_attn": g_attn, "w_out": w_out, "norm_post_g": norm_post_g}


def reference(x, norm_pre_g, w_in, b_forget, ssm_a_re, ssm_a_im, ssm_log_dt,
              ssm_b_re, ssm_b_im, ssm_c_re, ssm_c_im, ssm_d, w_glu, b_glu,
              g_ssm, g_attn, w_out, norm_post_g):
    h = x
    for layer in range(DEPTH):
        h = _hybrid_layer(h, norm_pre_g[layer], w_in[layer], b_forget[layer],
                          ssm_a_re[layer], ssm_a_im[layer], ssm_log_dt[layer],
                          ssm_b_re[layer], ssm_b_im[layer], ssm_c_re[layer], ssm_c_im[layer],
                          ssm_d[layer], w_glu[layer], b_glu[layer], g_ssm[layer], g_attn[layer],
                          w_out[layer], norm_post_g[layer])
    return h
```

```python
import functools
import math

import jax
import jax.numpy as jnp
from jax import lax
from jax.experimental import pallas as pl
from jax.experimental.pallas import tpu as pltpu

F32 = jnp.float32
BF16 = jnp.bfloat16

SSM_GROUP_CH = 16
SSM_STATE = 64
HEAD_DIM = 64
RMS_EPS = 1e-6

LANES = 128
SUBLANES = 8
VMEM_LIMIT_BYTES = 56 * 1024 * 1024

TM_PROJ = 512
TL_SSM = 256
TQ_ATTN = 512
SG_GROUPS = LANES // SSM_GROUP_CH
SG_STATE = SG_GROUPS * SSM_STATE
SCAN_W = 256
NEG_BIG = -1e30


def _cparams(*sem):
    return pltpu.CompilerParams(dimension_semantics=sem, vmem_limit_bytes=VMEM_LIMIT_BYTES)


def _rms(v, g):
    return v * lax.rsqrt(jnp.mean(v * v, axis=-1, keepdims=True) + RMS_EPS) * g


def _in_proj_kernel(x_ref, g_ref, w_ref, wf_ref, uz_ref, q_ref, k_ref, v_ref, za_ref, ft_ref,
                    *, ssm_w, attn_w, n_heads):
    xb = _rms(x_ref[...], g_ref[...]).astype(BF16)
    c0 = 2 * ssm_w
    uz_ref[...] = jnp.dot(xb, w_ref[:, :c0], preferred_element_type=F32).astype(BF16)
    q = jnp.dot(xb, w_ref[:, c0:c0 + attn_w], preferred_element_type=F32)
    q_ref[...] = (q * HEAD_DIM ** -0.5).astype(BF16)
    k_ref[...] = jnp.dot(xb, w_ref[:, c0 + attn_w:c0 + 2 * attn_w],
                         preferred_element_type=F32).astype(BF16)
    v_ref[...] = jnp.dot(xb, w_ref[:, c0 + 2 * attn_w:c0 + 3 * attn_w],
                         preferred_element_type=F32).astype(BF16)
    za_ref[...] = jnp.dot(xb, w_ref[:, c0 + 3 * attn_w:c0 + 4 * attn_w],
                          preferred_element_type=F32).astype(BF16)
    ft = lax.dot_general(wf_ref[...], xb, (((1,), (1,)), ((), ())), preferred_element_type=F32)
    ft_ref[0] = ft[:n_heads]


def _forget_cum_kernel(ft_ref, bf_ref, cum_ref, *, seq):
    f = ft_ref[0] + bf_ref[...]
    lf = jnp.minimum(f, 0.0) - jnp.log1p(jnp.exp(-jnp.abs(f)))
    lane = lax.broadcasted_iota(jnp.int32, lf.shape, 1)
    k = 1
    while k < seq:
        lf = lf + jnp.where(lane >= k, pltpu.roll(lf, k, axis=1), 0.0)
        k *= 2
    cum_ref[0] = lf


def _s5_kernel(uz_ref, bsg_ref, csg_ref, coef_ref, d_ref, wglu_ref, bglu_ref, gs_ref,
               ys_ref, bu_ref, carry_ref, *, ssm_w, tl):
    n_sg = ssm_w // LANES
    sg_cols = 2 * SG_STATE

    @pl.when(pl.program_id(1) == 0)
    def _():
        carry_ref[...] = jnp.zeros_like(carry_ref)

    for i in range(n_sg):
        bu_ref[:, i * sg_cols:(i + 1) * sg_cols] = jnp.dot(
            uz_ref[:, i * LANES:(i + 1) * LANES], bsg_ref[i], preferred_element_type=F32)

    for i in range(n_sg):
        for half in range(SG_STATE // SCAN_W):
            cc0 = i * SG_STATE + half * SCAN_W
            re0 = i * sg_cols + half * SCAN_W
            im0 = re0 + SG_STATE
            cs = slice(cc0, cc0 + SCAN_W)
            steps = [(coef_ref[2 * j, :, cs], coef_ref[2 * j + 1, :, cs], 1 << j) for j in range(3)]
            pr, pi = coef_ref[6, :, cs], coef_ref[7, :, cs]

            def blk(r, carry, re0=re0, im0=im0, steps=steps, pr=pr, pi=pi):
                cr, ci = carry
                row = pl.multiple_of(r * SUBLANES, SUBLANES)
                hr = bu_ref[pl.ds(row, SUBLANES), re0:re0 + SCAN_W]
                hi = bu_ref[pl.ds(row, SUBLANES), im0:im0 + SCAN_W]
                for ar, ai, k in steps:
                    sr = pltpu.roll(hr, k, axis=0)
                    si = pltpu.roll(hi, k, axis=0)
                    hr, hi = hr + (ar * sr - ai * si), hi + (ar * si + ai * sr)
                crb = jnp.broadcast_to(cr, hr.shape)
                cib = jnp.broadcast_to(ci, hi.shape)
                hr, hi = hr + (pr * crb - pi * cib), hi + (pr * cib + pi * crb)
                bu_ref[pl.ds(row, SUBLANES), re0:re0 + SCAN_W] = hr
                bu_ref[pl.ds(row, SUBLANES), im0:im0 + SCAN_W] = hi
                return hr[SUBLANES - 1:SUBLANES, :], hi[SUBLANES - 1:SUBLANES, :]

            cr, ci = lax.fori_loop(0, tl // SUBLANES, blk,
                                   (carry_ref[0:1, cs], carry_ref[1:2, cs]), unroll=4)
            carry_ref[0:1, cs] = cr
            carry_ref[1:2, cs] = ci

    y = jnp.concatenate(
        [jnp.dot(bu_ref[:, i * sg_cols:(i + 1) * sg_cols].astype(BF16), csg_ref[i],
                 preferred_element_type=F32) for i in range(n_sg)], axis=-1)
    u = uz_ref[:, :ssm_w].astype(F32)
    z = uz_ref[:, ssm_w:].astype(F32)
    y = jax.nn.gelu(y + d_ref[...] * u)
    gate = jnp.dot(y.astype(BF16), wglu_ref[...], preferred_element_type=F32) + bglu_ref[...]
    y = y * jax.nn.sigmoid(gate)
    y = y * (z * jax.nn.sigmoid(z))
    ys_ref[...] = _rms(y, gs_ref[...]).astype(BF16)


def _attn_kernel(q_ref, k_ref, v_ref, cq_ref, ck_ref, o_ref, m_ref, l_ref, acc_ref, *, tq):
    qi = pl.program_id(2)
    q2 = q_ref[0]
    lane = lax.broadcasted_iota(jnp.int32, (1, LANES), 1)
    qpos = lax.broadcasted_iota(jnp.int32, (tq, tq), 0)
    kpos = lax.broadcasted_iota(jnp.int32, (tq, tq), 1)
    nt = (((1,), (1,)), ((), ()))
    outs = []
    for h in range(LANES // HEAD_DIM):
        in_head = (lane >= h * HEAD_DIM) & (lane < (h + 1) * HEAD_DIM)
        qm = jnp.where(in_head, q2, jnp.zeros_like(q2))
        cq = cq_ref[0, 0][:, h:h + 1]

        def scores(kj, qm=qm, cq=cq, h=h):
            start = pl.multiple_of(kj * tq, tq)
            kt = k_ref[0, pl.ds(start, tq), :]
            s = lax.dot_general(qm, kt, nt, preferred_element_type=F32)
            return s + cq - ck_ref[0, 0, h:h + 1, pl.ds(start, tq)], start

        s, start = scores(qi)
        s = jnp.where(kpos <= qpos, s, NEG_BIG)
        m = jnp.max(s, axis=-1, keepdims=True)
        p = jnp.exp(s - m)
        m_ref[...] = m
        l_ref[...] = jnp.sum(p, axis=-1, keepdims=True)
        acc_ref[...] = jnp.dot(p.astype(BF16), v_ref[0, pl.ds(start, tq), :],
                               preferred_element_type=F32)

        def body(kj, c, scores=scores):
            s, start = scores(kj)
            m_old = m_ref[...]
            m_new = jnp.maximum(m_old, jnp.max(s, axis=-1, keepdims=True))
            p = jnp.exp(s - m_new)
            alpha = jnp.exp(m_old - m_new)
            m_ref[...] = m_new
            l_ref[...] = alpha * l_ref[...] + jnp.sum(p, axis=-1, keepdims=True)
            acc_ref[...] = alpha * acc_ref[...] + jnp.dot(
                p.astype(BF16), v_ref[0, pl.ds(start, tq), :], preferred_element_type=F32)
            return c

        lax.fori_loop(0, qi, body, 0)
        outs.append((in_head, acc_ref[...] * (1.0 / l_ref[...])))
    o_ref[0] = jnp.where(outs[0][0], outs[0][1], outs[1][1])


def _out_proj_kernel(ys_ref, oa_ref, za_ref, ga_ref, w_ref, x_ref, gp_ref, out_ref, *, ssm_w):
    z = za_ref[...].astype(F32)
    ya = _rms(oa_ref[...] * (z * jax.nn.sigmoid(z)), ga_ref[...]).astype(BF16)
    y = (jnp.dot(ys_ref[...], w_ref[:ssm_w, :], preferred_element_type=F32)
         + jnp.dot(ya, w_ref[ssm_w:, :], preferred_element_type=F32))
    out_ref[...] = x_ref[...] + _rms(y, gp_ref[...])


def _s5_params(a_re, a_im, log_dt, b_re, b_im, c_re, c_im):
    g, n = a_re.shape
    h = b_re.shape[-1]
    n_sg = g // SG_GROUPS
    dt = jnp.exp(log_dt)[:, None]
    mag, ang = jnp.exp(dt * a_re), dt * a_im
    abar_re, abar_im = mag * jnp.cos(ang), mag * jnp.sin(ang)
    den = a_re * a_re + a_im * a_im
    nr, ni = abar_re - 1.0, abar_im
    coef_re = (nr * a_re + ni * a_im) / den
    coef_im = (ni * a_re - nr * a_im) / den
    bbar_re = coef_re[..., None] * b_re - coef_im[..., None] * b_im
    bbar_im = coef_re[..., None] * b_im + coef_im[..., None] * b_re
    eye = jnp.eye(SG_GROUPS, dtype=F32)
    bb = jnp.stack([bbar_re, bbar_im]).reshape(2, n_sg, SG_GROUPS, n, h)
    bsg = jnp.einsum('pignh,gk->ighpkn', bb, eye).reshape(n_sg, SG_GROUPS * h, 2 * SG_GROUPS * n)
    cc = jnp.stack([c_re, -c_im]).reshape(2, n_sg, SG_GROUPS, h, n)
    csg = jnp.einsum('pighn,gk->ipgnkh', cc, eye).reshape(n_sg, 2 * SG_GROUPS * n, SG_GROUPS * h)

    def apow(m):
        m = jnp.asarray(m, F32)[:, None, None]
        pm, pa = jnp.exp(m * (dt * a_re)), m * ang
        return (pm * jnp.cos(pa)).reshape(-1, g * n), (pm * jnp.sin(pa)).reshape(-1, g * n)

    rows = jnp.arange(SUBLANES)
    coef = []
    for k in (1, 2, 4):
        pr, pi = apow([k])
        live = (rows >= k)[:, None]
        coef += [jnp.where(live, pr, 0.0), jnp.where(live, pi, 0.0)]
    pr, pi = apow(rows + 1)
    coef += [pr, pi]
    return bsg.astype(BF16), csg.astype(BF16), jnp.stack(coef)


def _layer(x, norm_pre_g, w_in, b_forget, a_re, a_im, log_dt, b_re, b_im, c_re, c_im, d_skip,
           w_glu, b_glu, g_ssm, g_attn, w_out, norm_post_g):
    bsz, seq, d_model = x.shape
    n_heads = b_forget.shape[0]
    attn_w = n_heads * HEAD_DIM
    ssm_w = d_skip.shape[0]
    tokens = bsz * seq
    tm, tl, tq = TM_PROJ, TL_SSM, TQ_ATTN
    assert seq % tm == 0 and seq % tl == 0 and seq % tq == 0
    assert ssm_w % LANES == 0 and attn_w % LANES == 0 and n_heads <= 2 * SUBLANES
    nt = seq // tm
    x2 = x.reshape(tokens, d_model)
    row = lambda v: v.reshape(1, -1).astype(F32)

    c_main = 2 * ssm_w + 4 * attn_w
    w_main = w_in[:, :c_main].astype(BF16)
    wf_t = jnp.zeros((2 * SUBLANES, d_model), BF16).at[:n_heads].set(w_in[:, c_main:].T.astype(BF16))

    uz, q, k, v, za, ft = pl.pallas_call(
        functools.partial(_in_proj_kernel, ssm_w=ssm_w, attn_w=attn_w, n_heads=n_heads),
        grid=(tokens // tm,),
        in_specs=[pl.BlockSpec((tm, d_model), lambda i: (i, 0)),
                  pl.BlockSpec((1, d_model), lambda i: (0, 0)),
                  pl.BlockSpec((d_model, c_main), lambda i: (0, 0)),
                  pl.BlockSpec((2 * SUBLANES, d_model), lambda i: (0, 0))],
        out_specs=[pl.BlockSpec((tm, 2 * ssm_w), lambda i: (i, 0)),
                   pl.BlockSpec((tm, attn_w), lambda i: (i, 0)),
                   pl.BlockSpec((tm, attn_w), lambda i: (i, 0)),
                   pl.BlockSpec((tm, attn_w), lambda i: (i, 0)),
                   pl.BlockSpec((tm, attn_w), lambda i: (i, 0)),
                   pl.BlockSpec((1, n_heads, tm), lambda i: (i // nt, 0, i % nt))],
        out_shape=[jax.ShapeDtypeStruct((tokens, 2 * ssm_w), BF16),
                   jax.ShapeDtypeStruct((tokens, attn_w), BF16),
                   jax.ShapeDtypeStruct((tokens, attn_w), BF16),
                   jax.ShapeDtypeStruct((tokens, attn_w), BF16),
                   jax.ShapeDtypeStruct((tokens, attn_w), BF16),
                   jax.ShapeDtypeStruct((bsz, n_heads, seq), F32)],
        compiler_params=_cparams("arbitrary"),
        name="in_proj",
    )(x2, row(norm_pre_g), w_main, wf_t)

    cum = pl.pallas_call(
        functools.partial(_forget_cum_kernel, seq=seq),
        grid=(bsz,),
        in_specs=[pl.BlockSpec((1, n_heads, seq), lambda b: (b, 0, 0)),
                  pl.BlockSpec((n_heads, 1), lambda b: (0, 0))],
        out_specs=pl.BlockSpec((1, n_heads, seq), lambda b: (b, 0, 0)),
        out_shape=jax.ShapeDtypeStruct((bsz, n_heads, seq), F32),
        compiler_params=_cparams("arbitrary"),
        name="forget_cum",
    )(ft, b_forget.reshape(n_heads, 1).astype(F32))

    bsg, csg, coef = _s5_params(a_re, a_im, log_dt, b_re, b_im, c_re, c_im)
    n_sg = ssm_w // LANES
    n_state = a_re.size
    nl = seq // tl
    ys = pl.pallas_call(
        functools.partial(_s5_kernel, ssm_w=ssm_w, tl=tl),
        grid=(bsz, nl),
        in_specs=[pl.BlockSpec((tl, 2 * ssm_w), lambda b, t: (b * nl + t, 0)),
                  pl.BlockSpec(bsg.shape, lambda b, t: (0, 0, 0)),
                  pl.BlockSpec(csg.shape, lambda b, t: (0, 0, 0)),
                  pl.BlockSpec(coef.shape, lambda b, t: (0, 0, 0)),
                  pl.BlockSpec((1, ssm_w), lambda b, t: (0, 0)),
                  pl.BlockSpec((ssm_w, ssm_w), lambda b, t: (0, 0)),
                  pl.BlockSpec((1, ssm_w), lambda b, t: (0, 0)),
                  pl.BlockSpec((1, ssm_w), lambda b, t: (0, 0))],
        out_specs=pl.BlockSpec((tl, ssm_w), lambda b, t: (b * nl + t, 0)),
        out_shape=jax.ShapeDtypeStruct((tokens, ssm_w), BF16),
        scratch_shapes=[pltpu.VMEM((tl, 2 * n_state), F32),
                        pltpu.VMEM((2, n_state), F32)],
        compiler_params=_cparams("arbitrary", "arbitrary"),
        name="s5",
    )(uz, bsg, csg, coef, row(d_skip), w_glu.astype(BF16), row(b_glu), row(g_ssm))

    hp = LANES // HEAD_DIM
    n_hp = n_heads // hp
    ck = cum.reshape(bsz, n_hp, hp, seq)
    cq = ck.transpose(0, 1, 3, 2)
    nq = seq // tq
    oa = pl.pallas_call(
        functools.partial(_attn_kernel, tq=tq),
        grid=(bsz, n_hp, nq),
        in_specs=[pl.BlockSpec((1, tq, LANES), lambda b, p, i: (b, i, p)),
                  pl.BlockSpec((1, seq, LANES), lambda b, p, i: (b, 0, p)),
                  pl.BlockSpec((1, seq, LANES), lambda b, p, i: (b, 0, p)),
                  pl.BlockSpec((1, 1, tq, hp), lambda b, p, i: (b, p, i, 0)),
                  pl.BlockSpec((1, 1, hp, seq), lambda b, p, i: (b, p, 0, 0))],
        out_specs=pl.BlockSpec((1, tq, LANES), lambda b, p, i: (b, i, p)),
        out_shape=jax.ShapeDtypeStruct((bsz, seq, attn_w), F32),
        scratch_shapes=[pltpu.VMEM((tq, 1), F32), pltpu.VMEM((tq, 1), F32),
                        pltpu.VMEM((tq, LANES), F32)],
        compiler_params=_cparams("arbitrary", "arbitrary", "arbitrary"),
        name="fox_attn",
    )(q.reshape(bsz, seq, attn_w), k.reshape(bsz, seq, attn_w), v.reshape(bsz, seq, attn_w), cq, ck)

    out = pl.pallas_call(
        functools.partial(_out_proj_kernel, ssm_w=ssm_w),
        grid=(tokens // tm,),
        in_specs=[pl.BlockSpec((tm, ssm_w), lambda i: (i, 0)),
                  pl.BlockSpec((tm, attn_w), lambda i: (i, 0)),
                  pl.BlockSpec((tm, attn_w), lambda i: (i, 0)),
                  pl.BlockSpec((1, attn_w), lambda i: (0, 0)),
                  pl.BlockSpec((ssm_w + attn_w, d_model), lambda i: (0, 0)),
                  pl.BlockSpec((tm, d_model), lambda i: (i, 0)),
                  pl.BlockSpec((1, d_model), lambda i: (0, 0))],
        out_specs=pl.BlockSpec((tm, d_model), lambda i: (i, 0)),
        out_shape=jax.ShapeDtypeStruct((tokens, d_model), F32),
        compiler_params=_cparams("arbitrary"),
        name="out_proj",
    )(ys, oa.reshape(tokens, attn_w), za, row(g_attn), w_out.astype(BF16), x2, row(norm_post_g))
    return out.reshape(bsz, seq, d_model)


def kernel(x, norm_pre_g, w_in, b_forget, ssm_a_re, ssm_a_im, ssm_log_dt, ssm_b_re, ssm_b_im,
           ssm_c_re, ssm_c_im, ssm_d, w_glu, b_glu, g_ssm, g_attn, w_out, norm_post_g):
    h = x
    for layer in range(norm_pre_g.shape[0]):
        h = _layer(h, norm_pre_g[layer], w_in[layer], b_forget[layer], ssm_a_re[layer],
                   ssm_a_im[layer], ssm_log_dt[layer], ssm_b_re[layer], ssm_b_im[layer],
                   ssm_c_re[layer], ssm_c_im[layer], ssm_d[layer], w_glu[layer], b_glu[layer],
                   g_ssm[layer], g_attn[layer], w_out[layer], norm_post_g[layer])
    return h
```

```python
import functools
import math

import jax
import jax.numpy as jnp
from jax import lax
from jax.experimental import pallas as pl
from jax.experimental.pallas import tpu as pltpu

F32 = jnp.float32
BF16 = jnp.bfloat16

SSM_GROUP_CH = 16
SSM_STATE = 64
HEAD_DIM = 64
RMS_EPS = 1e-6

LANES = 128
SUBLANES = 8
VMEM_LIMIT_BYTES = 56 * 1024 * 1024

TM_PROJ = 512
TL_SSM = 256
TQ_ATTN = 512
TK_ATTN = 256
CUM_CHUNK = 512
LOG2E = math.log2(math.e)
NT_DIMS = (((1,), (1,)), ((), ()))
SG_GROUPS = LANES // SSM_GROUP_CH
SG_STATE = SG_GROUPS * SSM_STATE
SCAN_W = 256
NEG_BIG = -1e30


def _cparams(*sem):
    return pltpu.CompilerParams(dimension_semantics=sem, vmem_limit_bytes=VMEM_LIMIT_BYTES)


def _rms(v, g):
    return v * lax.rsqrt(jnp.mean(v * v, axis=-1, keepdims=True) + RMS_EPS) * g


def _in_proj_kernel(x_ref, g_ref, w_ref, wt_ref, uz_ref, k_ref, za_ref, qt_ref, vt_ref, ft_ref,
                    *, ssm_w, attn_w, n_heads):
    xb = _rms(x_ref[...], g_ref[...]).astype(BF16)
    c0 = 2 * ssm_w
    uz_ref[...] = jnp.dot(xb, w_ref[:, :c0], preferred_element_type=F32).astype(BF16)
    k_ref[...] = jnp.dot(xb, w_ref[:, c0:c0 + attn_w], preferred_element_type=F32).astype(BF16)
    za_ref[...] = jnp.dot(xb, w_ref[:, c0 + attn_w:], preferred_element_type=F32).astype(BF16)
    qt = lax.dot_general(wt_ref[:attn_w], xb, NT_DIMS, preferred_element_type=F32)
    qt_ref[0] = (qt * (HEAD_DIM ** -0.5 * LOG2E)).astype(BF16)
    vt_ref[0] = lax.dot_general(wt_ref[attn_w:2 * attn_w], xb, NT_DIMS,
                                preferred_element_type=F32).astype(BF16)
    ft = lax.dot_general(wt_ref[2 * attn_w:], xb, NT_DIMS, preferred_element_type=F32)
    ft_ref[0] = ft[:n_heads]


def _forget_cum_kernel(ft_ref, bf_ref, aug_ref, *, seq, n_hp, chunk):
    f = ft_ref[0] + bf_ref[...]
    lf = jnp.minimum(f, 0.0) - jnp.log1p(jnp.exp(-jnp.abs(f)))
    lane = lax.broadcasted_iota(jnp.int32, lf.shape, 1)
    k = 1
    while k < seq:
        lf = lf + jnp.where(lane >= k, pltpu.roll(lf, k, axis=1), 0.0)
        k *= 2
    nb = lf * (-LOG2E)
    hi = nb.astype(BF16).astype(F32)
    mid = (nb - hi).astype(BF16).astype(F32)
    lo = (nb - hi - mid).astype(BF16).astype(F32)
    parts = (hi, mid, lo)
    rid = lax.broadcasted_iota(jnp.int32, (SUBLANES, seq), 0)
    eye = (lax.broadcasted_iota(jnp.int32, (chunk, chunk), 0)
           == lax.broadcasted_iota(jnp.int32, (chunk, chunk), 1)).astype(BF16)
    pad = jnp.zeros((LANES - SUBLANES, seq), F32)
    for p in range(n_hp):
        rows = jnp.zeros((SUBLANES, seq), F32)
        for j in range(LANES // HEAD_DIM):
            for t, part in enumerate(parts):
                src = part[2 * p + j:2 * p + j + 1, :]
                rows = jnp.where(rid == 3 * j + t, jnp.broadcast_to(src, rows.shape), rows)
        bmat = jnp.concatenate([rows, pad], axis=0).astype(BF16)
        for c in range(seq // chunk):
            blk = lax.dot_general(eye, bmat[:, c * chunk:(c + 1) * chunk], NT_DIMS,
                                  preferred_element_type=F32)
            aug_ref[0, p, c * chunk:(c + 1) * chunk, :] = blk.astype(BF16)


def _s5_kernel(uz_ref, bsg_ref, csg_ref, coef_ref, d_ref, wglu_ref, bglu_ref, gs_ref,
               ys_ref, bu_ref, carry_ref, *, ssm_w, tl):
    n_sg = ssm_w // LANES
    sg_cols = 2 * SG_STATE

    @pl.when(pl.program_id(1) == 0)
    def _():
        carry_ref[...] = jnp.zeros_like(carry_ref)

    for i in range(n_sg):
        bu_ref[:, i * sg_cols:(i + 1) * sg_cols] = jnp.dot(
            uz_ref[:, i * LANES:(i + 1) * LANES], bsg_ref[i], preferred_element_type=F32)

    for i in range(n_sg):
        for half in range(SG_STATE // SCAN_W):
            cc0 = i * SG_STATE + half * SCAN_W
            re0 = i * sg_cols + half * SCAN_W
            im0 = re0 + SG_STATE
            cs = slice(cc0, cc0 + SCAN_W)
            steps = [(coef_ref[2 * j, :, cs], coef_ref[2 * j + 1, :, cs], 1 << j) for j in range(3)]
            pr, pi = coef_ref[6, :, cs], coef_ref[7, :, cs]

            def blk(r, carry, re0=re0, im0=im0, steps=steps, pr=pr, pi=pi):
                cr, ci = carry
                row = pl.multiple_of(r * SUBLANES, SUBLANES)
                hr = bu_ref[pl.ds(row, SUBLANES), re0:re0 + SCAN_W]
                hi = bu_ref[pl.ds(row, SUBLANES), im0:im0 + SCAN_W]
                for ar, ai, k in steps:
                    sr = pltpu.roll(hr, k, axis=0)
                    si = pltpu.roll(hi, k, axis=0)
                    hr, hi = hr + (ar * sr - ai * si), hi + (ar * si + ai * sr)
                crb = jnp.broadcast_to(cr, hr.shape)
                cib = jnp.broadcast_to(ci, hi.shape)
                hr, hi = hr + (pr * crb - pi * cib), hi + (pr * cib + pi * crb)
                bu_ref[pl.ds(row, SUBLANES), re0:re0 + SCAN_W] = hr
                bu_ref[pl.ds(row, SUBLANES), im0:im0 + SCAN_W] = hi
                return hr[SUBLANES - 1:SUBLANES, :], hi[SUBLANES - 1:SUBLANES, :]

            cr, ci = lax.fori_loop(0, tl // SUBLANES, blk,
                                   (carry_ref[0:1, cs], carry_ref[1:2, cs]), unroll=4)
            carry_ref[0:1, cs] = cr
            carry_ref[1:2, cs] = ci

    y = jnp.concatenate(
        [jnp.dot(bu_ref[:, i * sg_cols:(i + 1) * sg_cols].astype(BF16), csg_ref[i],
                 preferred_element_type=F32) for i in range(n_sg)], axis=-1)
    u = uz_ref[:, :ssm_w].astype(F32)
    z = uz_ref[:, ssm_w:].astype(F32)
    y = jax.nn.gelu(y + d_ref[...] * u)
    gate = jnp.dot(y.astype(BF16), wglu_ref[...], preferred_element_type=F32) + bglu_ref[...]
    y = y * jax.nn.sigmoid(gate)
    y = y * (z * jax.nn.sigmoid(z))
    ys_ref[...] = _rms(y, gs_ref[...]).astype(BF16)


def _attn_kernel(qt_ref, k_ref, aug_ref, vt_ref, o_ref, qc_ref, sa_ref, sb_ref, m_ref, l_ref,
                 acc_ref, *, tq, tk):
    qi = pl.program_id(2)
    n_h = LANES // HEAD_DIM

    zero = jnp.zeros((HEAD_DIM, tq), BF16)
    for h in range(n_h):
        for hh in range(n_h):
            qc_ref[h * HEAD_DIM:(h + 1) * HEAD_DIM, hh * tq:(hh + 1) * tq] = (
                qt_ref[0, h * HEAD_DIM:(h + 1) * HEAD_DIM, :] if h == hh else zero)
    r = lax.broadcasted_iota(jnp.int32, (LANES, n_h * tq), 0)
    c = lax.broadcasted_iota(jnp.int32, (LANES, n_h * tq), 1)
    ones = (r >= 0) & (r < 3) & (c < tq)
    for h in range(1, n_h):
        ones = ones | ((r >= 3 * h) & (r < 3 * h + 3) & (c >= h * tq) & (c < (h + 1) * tq))
    qc_ref[LANES:, :] = jnp.where(ones, 1.0, 0.0).astype(BF16)

    def qk(start, size):
        lhs = jnp.concatenate([k_ref[0, pl.ds(start, size), :],
                               aug_ref[0, 0, pl.ds(start, size), :]], axis=1)
        return jnp.dot(lhs, qc_ref[...], preferred_element_type=F32)

    def step(s, start, size, first):
        m_loc = jnp.max(s, axis=0, keepdims=True)
        if first:
            m_new = m_loc
        else:
            m_old = m_ref[...]
            m_new = jnp.maximum(m_old, m_loc)
            alpha = jnp.exp2(m_old - m_new)
        p = jnp.exp2(s - m_new)
        psum = jnp.sum(p, axis=0, keepdims=True)
        pb = p.astype(BF16)
        m_ref[...] = m_new
        l_ref[...] = psum if first else alpha * l_ref[...] + psum
        for h in range(n_h):
            rows = slice(h * HEAD_DIM, (h + 1) * HEAD_DIM)
            cols = slice(h * tq, (h + 1) * tq)
            pv = jnp.dot(vt_ref[0, rows, pl.ds(start, size)], pb[:, cols],
                         preferred_element_type=F32)
            acc_ref[rows, :] = pv if first else alpha[:, cols] * acc_ref[rows, :] + pv

    d0 = pl.multiple_of(qi * tq, tq)
    sa_ref[...] = qk(0, tk)
    s = qk(d0, tq)
    kr = lax.broadcasted_iota(jnp.int32, s.shape, 0)
    qcol = lax.broadcasted_iota(jnp.int32, s.shape, 1)
    qcol = qcol - (qcol // tq) * tq
    step(jnp.where(kr <= qcol, s, NEG_BIG), d0, tq, True)

    def pair(jj, carry):
        base = pl.multiple_of(jj * tq, tq)
        sb_ref[...] = qk(base + tk, tk)
        step(sa_ref[...], base, tk, False)
        nxt = pl.multiple_of(jnp.minimum(jj + 1, qi - 1) * tq, tq)
        sa_ref[...] = qk(nxt, tk)
        step(sb_ref[...], base + tk, tk, False)
        return carry

    lax.fori_loop(0, qi, pair, 0)

    inv = 1.0 / l_ref[...]
    ot = jnp.concatenate(
        [acc_ref[h * HEAD_DIM:(h + 1) * HEAD_DIM, :] * inv[:, h * tq:(h + 1) * tq]
         for h in range(n_h)], axis=0)
    o_ref[0] = ot.T


def _out_proj_kernel(ys_ref, oa_ref, za_ref, ga_ref, w_ref, x_ref, gp_ref, out_ref, *, ssm_w):
    z = za_ref[...].astype(F32)
    ya = _rms(oa_ref[...] * (z * jax.nn.sigmoid(z)), ga_ref[...]).astype(BF16)
    y = (jnp.dot(ys_ref[...], w_ref[:ssm_w, :], preferred_element_type=F32)
         + jnp.dot(ya, w_ref[ssm_w:, :], preferred_element_type=F32))
    out_ref[...] = x_ref[...] + _rms(y, gp_ref[...])


def _s5_params(a_re, a_im, log_dt, b_re, b_im, c_re, c_im):
    g, n = a_re.shape
    h = b_re.shape[-1]
    n_sg = g // SG_GROUPS
    dt = jnp.exp(log_dt)[:, None]
    mag, ang = jnp.exp(dt * a_re), dt * a_im
    abar_re, abar_im = mag * jnp.cos(ang), mag * jnp.sin(ang)
    den = a_re * a_re + a_im * a_im
    nr, ni = abar_re - 1.0, abar_im
    coef_re = (nr * a_re + ni * a_im) / den
    coef_im = (ni * a_re - nr * a_im) / den
    bbar_re = coef_re[..., None] * b_re - coef_im[..., None] * b_im
    bbar_im = coef_re[..., None] * b_im + coef_im[..., None] * b_re
    eye = jnp.eye(SG_GROUPS, dtype=F32)
    bb = jnp.stack([bbar_re, bbar_im]).reshape(2, n_sg, SG_GROUPS, n, h)
    bsg = jnp.einsum('pignh,gk->ighpkn', bb, eye).reshape(n_sg, SG_GROUPS * h, 2 * SG_GROUPS * n)
    cc = jnp.stack([c_re, -c_im]).reshape(2, n_sg, SG_GROUPS, h, n)
    csg = jnp.einsum('pighn,gk->ipgnkh', cc, eye).reshape(n_sg, 2 * SG_GROUPS * n, SG_GROUPS * h)

    def apow(m):
        m = jnp.asarray(m, F32)[:, None, None]
        pm, pa = jnp.exp(m * (dt * a_re)), m * ang
        return (pm * jnp.cos(pa)).reshape(-1, g * n), (pm * jnp.sin(pa)).reshape(-1, g * n)

    rows = jnp.arange(SUBLANES)
    coef = []
    for k in (1, 2, 4):
        pr, pi = apow([k])
        live = (rows >= k)[:, None]
        coef += [jnp.where(live, pr, 0.0), jnp.where(live, pi, 0.0)]
    pr, pi = apow(rows + 1)
    coef += [pr, pi]
    return bsg.astype(BF16), csg.astype(BF16), jnp.stack(coef)


def _layer(x, norm_pre_g, w_in, b_forget, a_re, a_im, log_dt, b_re, b_im, c_re, c_im, d_skip,
           w_glu, b_glu, g_ssm, g_attn, w_out, norm_post_g):
    bsz, seq, d_model = x.shape
    n_heads = b_forget.shape[0]
    attn_w = n_heads * HEAD_DIM
    ssm_w = d_skip.shape[0]
    tokens = bsz * seq
    tm, tl, tq = TM_PROJ, TL_SSM, TQ_ATTN
    assert seq % tm == 0 and seq % tl == 0 and seq % tq == 0
    assert ssm_w % LANES == 0 and attn_w % LANES == 0 and n_heads <= 2 * SUBLANES
    nt = seq // tm
    x2 = x.reshape(tokens, d_model)
    row = lambda v: v.reshape(1, -1).astype(F32)

    c0 = 2 * ssm_w
    w_q, w_k, w_v, w_za, w_f = (w_in[:, c0 + j * attn_w:c0 + (j + 1) * attn_w] for j in range(5))
    w_main = jnp.concatenate([w_in[:, :c0], w_k, w_za], axis=1).astype(BF16)
    w_t = jnp.concatenate([w_q.T, w_v.T, w_f.T,
                           jnp.zeros((2 * SUBLANES - n_heads, d_model), F32)], axis=0).astype(BF16)

    uz, k, za, qt, vt, ft = pl.pallas_call(
        functools.partial(_in_proj_kernel, ssm_w=ssm_w, attn_w=attn_w, n_heads=n_heads),
        grid=(tokens // tm,),
        in_specs=[pl.BlockSpec((tm, d_model), lambda i: (i, 0)),
                  pl.BlockSpec((1, d_model), lambda i: (0, 0)),
                  pl.BlockSpec(w_main.shape, lambda i: (0, 0)),
                  pl.BlockSpec(w_t.shape, lambda i: (0, 0))],
        out_specs=[pl.BlockSpec((tm, 2 * ssm_w), lambda i: (i, 0)),
                   pl.BlockSpec((tm, attn_w), lambda i: (i, 0)),
                   pl.BlockSpec((tm, attn_w), lambda i: (i, 0)),
                   pl.BlockSpec((1, attn_w, tm), lambda i: (i // nt, 0, i % nt)),
                   pl.BlockSpec((1, attn_w, tm), lambda i: (i // nt, 0, i % nt)),
                   pl.BlockSpec((1, n_heads, tm), lambda i: (i // nt, 0, i % nt))],
        out_shape=[jax.ShapeDtypeStruct((tokens, 2 * ssm_w), BF16),
                   jax.ShapeDtypeStruct((tokens, attn_w), BF16),
                   jax.ShapeDtypeStruct((tokens, attn_w), BF16),
                   jax.ShapeDtypeStruct((bsz, attn_w, seq), BF16),
                   jax.ShapeDtypeStruct((bsz, attn_w, seq), BF16),
                   jax.ShapeDtypeStruct((bsz, n_heads, seq), F32)],
        compiler_params=_cparams("arbitrary"),
        name="in_proj",
    )(x2, row(norm_pre_g), w_main, w_t)

    n_hp = n_heads // (LANES // HEAD_DIM)
    aug = pl.pallas_call(
        functools.partial(_forget_cum_kernel, seq=seq, n_hp=n_hp, chunk=CUM_CHUNK),
        grid=(bsz,),
        in_specs=[pl.BlockSpec((1, n_heads, seq), lambda b: (b, 0, 0)),
                  pl.BlockSpec((n_heads, 1), lambda b: (0, 0))],
        out_specs=pl.BlockSpec((1, n_hp, seq, LANES), lambda b: (b, 0, 0, 0)),
        out_shape=jax.ShapeDtypeStruct((bsz, n_hp, seq, LANES), BF16),
        compiler_params=_cparams("arbitrary"),
        name="forget_cum",
    )(ft, b_forget.reshape(n_heads, 1).astype(F32))

    bsg, csg, coef = _s5_params(a_re, a_im, log_dt, b_re, b_im, c_re, c_im)
    n_sg = ssm_w // LANES
    n_state = a_re.size
    nl = seq // tl
    ys = pl.pallas_call(
        functools.partial(_s5_kernel, ssm_w=ssm_w, tl=tl),
        grid=(bsz, nl),
        in_specs=[pl.BlockSpec((tl, 2 * ssm_w), lambda b, t: (b * nl + t, 0)),
                  pl.BlockSpec(bsg.shape, lambda b, t: (0, 0, 0)),
                  pl.BlockSpec(csg.shape, lambda b, t: (0, 0, 0)),
                  pl.BlockSpec(coef.shape, lambda b, t: (0, 0, 0)),
                  pl.BlockSpec((1, ssm_w), lambda b, t: (0, 0)),
                  pl.BlockSpec((ssm_w, ssm_w), lambda b, t: (0, 0)),
                  pl.BlockSpec((1, ssm_w), lambda b, t: (0, 0)),
                  pl.BlockSpec((1, ssm_w), lambda b, t: (0, 0))],
        out_specs=pl.BlockSpec((tl, ssm_w), lambda b, t: (b * nl + t, 0)),
        out_shape=jax.ShapeDtypeStruct((tokens, ssm_w), BF16),
        scratch_shapes=[pltpu.VMEM((tl, 2 * n_state), F32),
                        pltpu.VMEM((2, n_state), F32)],
        compiler_params=_cparams("arbitrary", "arbitrary"),
        name="s5",
    )(uz, bsg, csg, coef, row(d_skip), w_glu.astype(BF16), row(b_glu), row(g_ssm))

    tk = TK_ATTN
    assert tq == 2 * tk and 3 * (LANES // HEAD_DIM) <= LANES
    nq = seq // tq
    wq = (LANES // HEAD_DIM) * tq
    oa = pl.pallas_call(
        functools.partial(_attn_kernel, tq=tq, tk=tk),
        grid=(bsz, n_hp, nq),
        in_specs=[pl.BlockSpec((1, LANES, tq), lambda b, p, i: (b, p, i)),
                  pl.BlockSpec((1, seq, LANES), lambda b, p, i: (b, 0, p)),
                  pl.BlockSpec((1, 1, seq, LANES), lambda b, p, i: (b, p, 0, 0)),
                  pl.BlockSpec((1, LANES, seq), lambda b, p, i: (b, p, 0))],
        out_specs=pl.BlockSpec((1, tq, LANES), lambda b, p, i: (b, i, p)),
        out_shape=jax.ShapeDtypeStruct((bsz, seq, attn_w), F32),
        scratch_shapes=[pltpu.VMEM((2 * LANES, wq), BF16),
                        pltpu.VMEM((tk, wq), F32),
                        pltpu.VMEM((tk, wq), F32),
                        pltpu.VMEM((1, wq), F32),
                        pltpu.VMEM((1, wq), F32),
                        pltpu.VMEM((LANES, tq), F32)],
        compiler_params=_cparams("arbitrary", "arbitrary", "arbitrary"),
        name="fox_attn",
    )(qt, k.reshape(bsz, seq, attn_w), aug, vt)

    out = pl.pallas_call(
        functools.partial(_out_proj_kernel, ssm_w=ssm_w),
        grid=(tokens // tm,),
        in_specs=[pl.BlockSpec((tm, ssm_w), lambda i: (i, 0)),
                  pl.BlockSpec((tm, attn_w), lambda i: (i, 0)),
                  pl.BlockSpec((tm, attn_w), lambda i: (i, 0)),
                  pl.BlockSpec((1, attn_w), lambda i: (0, 0)),
                  pl.BlockSpec((ssm_w + attn_w, d_model), lambda i: (0, 0)),
                  pl.BlockSpec((tm, d_model), lambda i: (i, 0)),
                  pl.BlockSpec((1, d_model), lambda i: (0, 0))],
        out_specs=pl.BlockSpec((tm, d_model), lambda i: (i, 0)),
        out_shape=jax.ShapeDtypeStruct((tokens, d_model), F32),
        compiler_params=_cparams("arbitrary"),
        name="out_proj",
    )(ys, oa.reshape(tokens, attn_w), za, row(g_attn), w_out.astype(BF16), x2, row(norm_post_g))
    return out.reshape(bsz, seq, d_model)


def kernel(x, norm_pre_g, w_in, b_forget, ssm_a_re, ssm_a_im, ssm_log_dt, ssm_b_re, ssm_b_im,
           ssm_c_re, ssm_c_im, ssm_d, w_glu, b_glu, g_ssm, g_attn, w_out, norm_post_g):
    h = x
    for layer in range(norm_pre_g.shape[0]):
        h = _layer(h, norm_pre_g[layer], w_in[layer], b_forget[layer], ssm_a_re[layer],
                   ssm_a_im[layer], ssm_log_dt[layer], ssm_b_re[layer], ssm_b_im[layer],
                   ssm_c_re[layer], ssm_c_im[layer], ssm_d[layer], w_glu[layer], b_glu[layer],
                   g_ssm[layer], g_attn[layer], w_out[layer], norm_post_g[layer])
    return h
```

```python
import functools
import math

import jax
import jax.numpy as jnp
from jax import lax
from jax.experimental import pallas as pl
from jax.experimental.pallas import tpu as pltpu

F32 = jnp.float32
BF16 = jnp.bfloat16

SSM_GROUP_CH = 16
SSM_STATE = 64
HEAD_DIM = 64
RMS_EPS = 1e-6

LANES = 128
SUBLANES = 8
VMEM_LIMIT_BYTES = 56 * 1024 * 1024

TM_PROJ = 512
TL_SSM = 256
TQ_ATTN = 512
TK_ATTN = 256
CUM_CHUNK = 512
LOG2E = math.log2(math.e)
NT_DIMS = (((1,), (1,)), ((), ()))
SG_GROUPS = LANES // SSM_GROUP_CH
SG_STATE = SG_GROUPS * SSM_STATE
SCAN_W = 256
NEG_BIG = -1e30


def _cparams(*sem):
    return pltpu.CompilerParams(dimension_semantics=sem, vmem_limit_bytes=VMEM_LIMIT_BYTES)


def _rms(v, g):
    return v * lax.rsqrt(jnp.mean(v * v, axis=-1, keepdims=True) + RMS_EPS) * g


def _in_proj_kernel(x_ref, g_ref, w_ref, wt_ref, uz_ref, k_ref, za_ref, qt_ref, vt_ref, ft_ref,
                    *, ssm_w, attn_w, n_heads):
    xb = _rms(x_ref[...], g_ref[...]).astype(BF16)
    c0 = 2 * ssm_w
    uz_ref[...] = jnp.dot(xb, w_ref[:, :c0], preferred_element_type=F32).astype(BF16)
    k_ref[...] = jnp.dot(xb, w_ref[:, c0:c0 + attn_w], preferred_element_type=F32).astype(BF16)
    za_ref[...] = jnp.dot(xb, w_ref[:, c0 + attn_w:], preferred_element_type=F32).astype(BF16)
    qt = lax.dot_general(wt_ref[:attn_w], xb, NT_DIMS, preferred_element_type=F32)
    qt_ref[0] = (qt * (HEAD_DIM ** -0.5 * LOG2E)).astype(BF16)
    vt_ref[0] = lax.dot_general(wt_ref[attn_w:2 * attn_w], xb, NT_DIMS,
                                preferred_element_type=F32).astype(BF16)
    ft = lax.dot_general(wt_ref[2 * attn_w:], xb, NT_DIMS, preferred_element_type=F32)
    ft_ref[0] = ft[:n_heads]


def _forget_cum_kernel(ft_ref, bf_ref, aug_ref, *, seq, n_hp, chunk):
    f = ft_ref[0] + bf_ref[...]
    lf = jnp.minimum(f, 0.0) - jnp.log1p(jnp.exp(-jnp.abs(f)))
    lane = lax.broadcasted_iota(jnp.int32, lf.shape, 1)
    k = 1
    while k < seq:
        lf = lf + jnp.where(lane >= k, pltpu.roll(lf, k, axis=1), 0.0)
        k *= 2
    nb = lf * (-LOG2E)
    hi = nb.astype(BF16).astype(F32)
    mid = (nb - hi).astype(BF16).astype(F32)
    lo = (nb - hi - mid).astype(BF16).astype(F32)
    parts = (hi, mid, lo)
    rid = lax.broadcasted_iota(jnp.int32, (SUBLANES, seq), 0)
    eye = (lax.broadcasted_iota(jnp.int32, (chunk, chunk), 0)
           == lax.broadcasted_iota(jnp.int32, (chunk, chunk), 1)).astype(BF16)
    pad = jnp.zeros((LANES - SUBLANES, seq), F32)
    for p in range(n_hp):
        rows = jnp.zeros((SUBLANES, seq), F32)
        for j in range(LANES // HEAD_DIM):
            for t, part in enumerate(parts):
                src = part[2 * p + j:2 * p + j + 1, :]
                rows = jnp.where(rid == 3 * j + t, jnp.broadcast_to(src, rows.shape), rows)
        bmat = jnp.concatenate([rows, pad], axis=0).astype(BF16)
        for c in range(seq // chunk):
            blk = lax.dot_general(eye, bmat[:, c * chunk:(c + 1) * chunk], NT_DIMS,
                                  preferred_element_type=F32)
            aug_ref[0, p, c * chunk:(c + 1) * chunk, :] = blk.astype(BF16)


def _cmul(ar, ai, xr, xi):
    return ar * xr - ai * xi, ar * xi + ai * xr


def _s5_kernel(uz_ref, bsg_ref, csg_ref, coef_ref, d_ref, wglu_ref, bglu_ref, gs_ref,
               ys_ref, bu_ref, hb_ref, carry_ref, *, ssm_w, tl):
    n_sg = ssm_w // LANES
    sg_cols = 2 * SG_STATE
    nj = tl // SUBLANES
    log_nj = nj.bit_length() - 1

    @pl.when(pl.program_id(1) == 0)
    def _():
        carry_ref[...] = jnp.zeros_like(carry_ref)

    ri = lax.broadcasted_iota(jnp.int32, (tl, tl), 0)
    ci = lax.broadcasted_iota(jnp.int32, (tl, tl), 1)
    log_sub = SUBLANES.bit_length() - 1
    perm = (ci == ((ri & (SUBLANES - 1)) << log_nj) + (ri >> log_sub)).astype(BF16)
    unperm = (ri == ((ci & (SUBLANES - 1)) << log_nj) + (ci >> log_sub)).astype(BF16)
    uzp = jnp.dot(perm, uz_ref[...], preferred_element_type=F32).astype(BF16)

    rid = lax.broadcasted_iota(jnp.int32, (SUBLANES, SG_STATE), 0)
    for i in range(n_sg):
        re = slice(i * sg_cols, i * sg_cols + SG_STATE)
        im = slice(i * sg_cols + SG_STATE, (i + 1) * sg_cols)
        cs = slice(i * SG_STATE, (i + 1) * SG_STATE)
        bu_ref[:, i * sg_cols:(i + 1) * sg_cols] = jnp.dot(
            uzp[:, i * LANES:(i + 1) * LANES], bsg_ref[i], preferred_element_type=F32)
        ar, ai = coef_ref[8, :, cs], coef_ref[9, :, cs]

        hr = jnp.zeros((SUBLANES, SG_STATE), F32)
        hi = jnp.zeros((SUBLANES, SG_STATE), F32)
        for j in range(nj):
            rows = slice(j * SUBLANES, (j + 1) * SUBLANES)
            tr, ti = _cmul(ar, ai, hr, hi)
            hr, hi = tr + bu_ref[rows, re], ti + bu_ref[rows, im]
            bu_ref[rows, re] = hr
            bu_ref[rows, im] = hi

        for t in range(3):
            k = 1 << t
            tr, ti = _cmul(coef_ref[2 * t, :, cs], coef_ref[2 * t + 1, :, cs],
                           pltpu.roll(hr, k, axis=0), pltpu.roll(hi, k, axis=0))
            hr, hi = hr + tr, hi + ti
        cr = jnp.broadcast_to(carry_ref[0:1, cs], hr.shape)
        cim = jnp.broadcast_to(carry_ref[1:2, cs], hi.shape)
        tr, ti = _cmul(coef_ref[6, :, cs], coef_ref[7, :, cs], cr, cim)
        hr, hi = hr + tr, hi + ti
        carry_ref[0:1, cs] = hr[SUBLANES - 1:SUBLANES, :]
        carry_ref[1:2, cs] = hi[SUBLANES - 1:SUBLANES, :]
        gr = jnp.where(rid == 0, cr, pltpu.roll(hr, 1, axis=0))
        gi = jnp.where(rid == 0, cim, pltpu.roll(hi, 1, axis=0))

        for j in range(0, nj, 2):
            outs = []
            for jj in (j, j + 1):
                rows = slice(jj * SUBLANES, (jj + 1) * SUBLANES)
                gr, gi = _cmul(ar, ai, gr, gi)
                outs.append((bu_ref[rows, re] + gr, bu_ref[rows, im] + gi))
            rows2 = slice(j * SUBLANES, (j + 2) * SUBLANES)
            hb_ref[rows2, re] = jnp.concatenate([outs[0][0], outs[1][0]], axis=0).astype(BF16)
            hb_ref[rows2, im] = jnp.concatenate([outs[0][1], outs[1][1]], axis=0).astype(BF16)

    y = jnp.concatenate(
        [jnp.dot(hb_ref[:, i * sg_cols:(i + 1) * sg_cols], csg_ref[i],
                 preferred_element_type=F32) for i in range(n_sg)], axis=-1)
    u = uzp[:, :ssm_w].astype(F32)
    z = uzp[:, ssm_w:].astype(F32)
    y = jax.nn.gelu(y + d_ref[...] * u)
    gate = jnp.dot(y.astype(BF16), wglu_ref[...], preferred_element_type=F32) + bglu_ref[...]
    y = y * jax.nn.sigmoid(gate)
    y = y * (z * jax.nn.sigmoid(z))
    ysp = _rms(y, gs_ref[...]).astype(BF16)
    ys_ref[...] = jnp.dot(unperm, ysp, preferred_element_type=F32).astype(BF16)


def _attn_kernel(qt_ref, k_ref, aug_ref, vt_ref, o_ref, qc_ref, sa_ref, sb_ref, m_ref, l_ref,
                 acc_ref, *, tq, tk):
    qi = pl.program_id(2)
    n_h = LANES // HEAD_DIM

    zero = jnp.zeros((HEAD_DIM, tq), BF16)
    for h in range(n_h):
        for hh in range(n_h):
            qc_ref[h * HEAD_DIM:(h + 1) * HEAD_DIM, hh * tq:(hh + 1) * tq] = (
                qt_ref[0, h * HEAD_DIM:(h + 1) * HEAD_DIM, :] if h == hh else zero)
    r = lax.broadcasted_iota(jnp.int32, (LANES, n_h * tq), 0)
    c = lax.broadcasted_iota(jnp.int32, (LANES, n_h * tq), 1)
    ones = (r >= 0) & (r < 3) & (c < tq)
    for h in range(1, n_h):
        ones = ones | ((r >= 3 * h) & (r < 3 * h + 3) & (c >= h * tq) & (c < (h + 1) * tq))
    qc_ref[LANES:, :] = jnp.where(ones, 1.0, 0.0).astype(BF16)

    def qk(start, size):
        lhs = jnp.concatenate([k_ref[0, pl.ds(start, size), :],
                               aug_ref[0, 0, pl.ds(start, size), :]], axis=1)
        return jnp.dot(lhs, qc_ref[...], preferred_element_type=F32)

    def step(s, start, size, first):
        m_loc = jnp.max(s, axis=0, keepdims=True)
        if first:
            m_new = m_loc
        else:
            m_old = m_ref[...]
            m_new = jnp.maximum(m_old, m_loc)
            alpha = jnp.exp2(m_old - m_new)
        p = jnp.exp2(s - m_new)
        psum = jnp.sum(p, axis=0, keepdims=True)
        pb = p.astype(BF16)
        m_ref[...] = m_new
        l_ref[...] = psum if first else alpha * l_ref[...] + psum
        for h in range(n_h):
            rows = slice(h * HEAD_DIM, (h + 1) * HEAD_DIM)
            cols = slice(h * tq, (h + 1) * tq)
            pv = jnp.dot(vt_ref[0, rows, pl.ds(start, size)], pb[:, cols],
                         preferred_element_type=F32)
            acc_ref[rows, :] = pv if first else alpha[:, cols] * acc_ref[rows, :] + pv

    d0 = pl.multiple_of(qi * tq, tq)
    sa_ref[...] = qk(0, tk)
    s = qk(d0, tq)
    kr = lax.broadcasted_iota(jnp.int32, s.shape, 0)
    qcol = lax.broadcasted_iota(jnp.int32, s.shape, 1)
    qcol = qcol - (qcol // tq) * tq
    step(jnp.where(kr <= qcol, s, NEG_BIG), d0, tq, True)

    def pair(jj, carry):
        base = pl.multiple_of(jj * tq, tq)
        sb_ref[...] = qk(base + tk, tk)
        step(sa_ref[...], base, tk, False)
        nxt = pl.multiple_of(jnp.minimum(jj + 1, qi - 1) * tq, tq)
        sa_ref[...] = qk(nxt, tk)
        step(sb_ref[...], base + tk, tk, False)
        return carry

    lax.fori_loop(0, qi, pair, 0)

    inv = 1.0 / l_ref[...]
    ot = jnp.concatenate(
        [acc_ref[h * HEAD_DIM:(h + 1) * HEAD_DIM, :] * inv[:, h * tq:(h + 1) * tq]
         for h in range(n_h)], axis=0)
    o_ref[0] = ot.T


def _out_proj_kernel(ys_ref, oa_ref, za_ref, ga_ref, w_ref, x_ref, gp_ref, out_ref, *, ssm_w):
    z = za_ref[...].astype(F32)
    ya = _rms(oa_ref[...] * (z * jax.nn.sigmoid(z)), ga_ref[...]).astype(BF16)
    y = (jnp.dot(ys_ref[...], w_ref[:ssm_w, :], preferred_element_type=F32)
         + jnp.dot(ya, w_ref[ssm_w:, :], preferred_element_type=F32))
    out_ref[...] = x_ref[...] + _rms(y, gp_ref[...])


def _s5_params(a_re, a_im, log_dt, b_re, b_im, c_re, c_im, nj):
    g, n = a_re.shape
    h = b_re.shape[-1]
    n_sg = g // SG_GROUPS
    dt = jnp.exp(log_dt)[:, None]
    mag, ang = jnp.exp(dt * a_re), dt * a_im
    abar_re, abar_im = mag * jnp.cos(ang), mag * jnp.sin(ang)
    den = a_re * a_re + a_im * a_im
    nr, ni = abar_re - 1.0, abar_im
    coef_re = (nr * a_re + ni * a_im) / den
    coef_im = (ni * a_re - nr * a_im) / den
    bbar_re = coef_re[..., None] * b_re - coef_im[..., None] * b_im
    bbar_im = coef_re[..., None] * b_im + coef_im[..., None] * b_re
    eye = jnp.eye(SG_GROUPS, dtype=F32)
    bb = jnp.stack([bbar_re, bbar_im]).reshape(2, n_sg, SG_GROUPS, n, h)
    bsg = jnp.einsum('pignh,gk->ighpkn', bb, eye).reshape(n_sg, SG_GROUPS * h, 2 * SG_GROUPS * n)
    cc = jnp.stack([c_re, -c_im]).reshape(2, n_sg, SG_GROUPS, h, n)
    csg = jnp.einsum('pighn,gk->ipgnkh', cc, eye).reshape(n_sg, 2 * SG_GROUPS * n, SG_GROUPS * h)

    def apow(m):
        m = jnp.asarray(m, F32)[:, None, None]
        pm, pa = jnp.exp(m * (dt * a_re)), m * ang
        return (pm * jnp.cos(pa)).reshape(-1, g * n), (pm * jnp.sin(pa)).reshape(-1, g * n)

    rows = jnp.arange(SUBLANES)
    coef = []
    for k in (1, 2, 4):
        pr, pi = apow([nj * k])
        live = (rows >= k)[:, None]
        coef += [jnp.where(live, pr, 0.0), jnp.where(live, pi, 0.0)]
    pr, pi = apow(nj * (rows + 1))
    coef += [pr, pi]
    pr, pi = apow(jnp.ones((SUBLANES,)))
    coef += [pr, pi]
    return bsg.astype(BF16), csg.astype(BF16), jnp.stack(coef)


def _layer(x, norm_pre_g, w_in, b_forget, a_re, a_im, log_dt, b_re, b_im, c_re, c_im, d_skip,
           w_glu, b_glu, g_ssm, g_attn, w_out, norm_post_g):
    bsz, seq, d_model = x.shape
    n_heads = b_forget.shape[0]
    attn_w = n_heads * HEAD_DIM
    ssm_w = d_skip.shape[0]
    tokens = bsz * seq
    tm, tl, tq = TM_PROJ, TL_SSM, TQ_ATTN
    assert seq % tm == 0 and seq % tl == 0 and seq % tq == 0
    assert ssm_w % LANES == 0 and attn_w % LANES == 0 and n_heads <= 2 * SUBLANES
    nt = seq // tm
    x2 = x.reshape(tokens, d_model)
    row = lambda v: v.reshape(1, -1).astype(F32)

    c0 = 2 * ssm_w
    w_q, w_k, w_v, w_za, w_f = (w_in[:, c0 + j * attn_w:c0 + (j + 1) * attn_w] for j in range(5))
    w_main = jnp.concatenate([w_in[:, :c0], w_k, w_za], axis=1).astype(BF16)
    w_t = jnp.concatenate([w_q.T, w_v.T, w_f.T,
                           jnp.zeros((2 * SUBLANES - n_heads, d_model), F32)], axis=0).astype(BF16)

    uz, k, za, qt, vt, ft = pl.pallas_call(
        functools.partial(_in_proj_kernel, ssm_w=ssm_w, attn_w=attn_w, n_heads=n_heads),
        grid=(tokens // tm,),
        in_specs=[pl.BlockSpec((tm, d_model), lambda i: (i, 0)),
                  pl.BlockSpec((1, d_model), lambda i: (0, 0)),
                  pl.BlockSpec(w_main.shape, lambda i: (0, 0)),
                  pl.BlockSpec(w_t.shape, lambda i: (0, 0))],
        out_specs=[pl.BlockSpec((tm, 2 * ssm_w), lambda i: (i, 0)),
                   pl.BlockSpec((tm, attn_w), lambda i: (i, 0)),
                   pl.BlockSpec((tm, attn_w), lambda i: (i, 0)),
                   pl.BlockSpec((1, attn_w, tm), lambda i: (i // nt, 0, i % nt)),
                   pl.BlockSpec((1, attn_w, tm), lambda i: (i // nt, 0, i % nt)),
                   pl.BlockSpec((1, n_heads, tm), lambda i: (i // nt, 0, i % nt))],
        out_shape=[jax.ShapeDtypeStruct((tokens, 2 * ssm_w), BF16),
                   jax.ShapeDtypeStruct((tokens, attn_w), BF16),
                   jax.ShapeDtypeStruct((tokens, attn_w), BF16),
                   jax.ShapeDtypeStruct((bsz, attn_w, seq), BF16),
                   jax.ShapeDtypeStruct((bsz, attn_w, seq), BF16),
                   jax.ShapeDtypeStruct((bsz, n_heads, seq), F32)],
        compiler_params=_cparams("arbitrary"),
        name="in_proj",
    )(x2, row(norm_pre_g), w_main, w_t)

    n_hp = n_heads // (LANES // HEAD_DIM)
    aug = pl.pallas_call(
        functools.partial(_forget_cum_kernel, seq=seq, n_hp=n_hp, chunk=CUM_CHUNK),
        grid=(bsz,),
        in_specs=[pl.BlockSpec((1, n_heads, seq), lambda b: (b, 0, 0)),
                  pl.BlockSpec((n_heads, 1), lambda b: (0, 0))],
        out_specs=pl.BlockSpec((1, n_hp, seq, LANES), lambda b: (b, 0, 0, 0)),
        out_shape=jax.ShapeDtypeStruct((bsz, n_hp, seq, LANES), BF16),
        compiler_params=_cparams("arbitrary"),
        name="forget_cum",
    )(ft, b_forget.reshape(n_heads, 1).astype(F32))

    assert tl % (2 * SUBLANES * SUBLANES) == 0 and (tl // SUBLANES) & (tl // SUBLANES - 1) == 0
    bsg, csg, coef = _s5_params(a_re, a_im, log_dt, b_re, b_im, c_re, c_im, tl // SUBLANES)
    n_sg = ssm_w // LANES
    n_state = a_re.size
    nl = seq // tl
    ys = pl.pallas_call(
        functools.partial(_s5_kernel, ssm_w=ssm_w, tl=tl),
        grid=(bsz, nl),
        in_specs=[pl.BlockSpec((tl, 2 * ssm_w), lambda b, t: (b * nl + t, 0)),
                  pl.BlockSpec(bsg.shape, lambda b, t: (0, 0, 0)),
                  pl.BlockSpec(csg.shape, lambda b, t: (0, 0, 0)),
                  pl.BlockSpec(coef.shape, lambda b, t: (0, 0, 0)),
                  pl.BlockSpec((1, ssm_w), lambda b, t: (0, 0)),
                  pl.BlockSpec((ssm_w, ssm_w), lambda b, t: (0, 0)),
                  pl.BlockSpec((1, ssm_w), lambda b, t: (0, 0)),
                  pl.BlockSpec((1, ssm_w), lambda b, t: (0, 0))],
        out_specs=pl.BlockSpec((tl, ssm_w), lambda b, t: (b * nl + t, 0)),
        out_shape=jax.ShapeDtypeStruct((tokens, ssm_w), BF16),
        scratch_shapes=[pltpu.VMEM((tl, 2 * n_state), F32),
                        pltpu.VMEM((tl, 2 * n_state), BF16),
                        pltpu.VMEM((2, n_state), F32)],
        compiler_params=_cparams("arbitrary", "arbitrary"),
        name="s5",
    )(uz, bsg, csg, coef, row(d_skip), w_glu.astype(BF16), row(b_glu), row(g_ssm))

    tk = TK_ATTN
    assert tq == 2 * tk and 3 * (LANES // HEAD_DIM) <= LANES
    nq = seq // tq
    wq = (LANES // HEAD_DIM) * tq
    oa = pl.pallas_call(
        functools.partial(_attn_kernel, tq=tq, tk=tk),
        grid=(bsz, n_hp, nq),
        in_specs=[pl.BlockSpec((1, LANES, tq), lambda b, p, i: (b, p, i)),
                  pl.BlockSpec((1, seq, LANES), lambda b, p, i: (b, 0, p)),
                  pl.BlockSpec((1, 1, seq, LANES), lambda b, p, i: (b, p, 0, 0)),
                  pl.BlockSpec((1, LANES, seq), lambda b, p, i: (b, p, 0))],
        out_specs=pl.BlockSpec((1, tq, LANES), lambda b, p, i: (b, i, p)),
        out_shape=jax.ShapeDtypeStruct((bsz, seq, attn_w), F32),
        scratch_shapes=[pltpu.VMEM((2 * LANES, wq), BF16),
                        pltpu.VMEM((tk, wq), F32),
                        pltpu.VMEM((tk, wq), F32),
                        pltpu.VMEM((1, wq), F32),
                        pltpu.VMEM((1, wq), F32),
                        pltpu.VMEM((LANES, tq), F32)],
        compiler_params=_cparams("arbitrary", "arbitrary", "arbitrary"),
        name="fox_attn",
    )(qt, k.reshape(bsz, seq, attn_w), aug, vt)

    out = pl.pallas_call(
        functools.partial(_out_proj_kernel, ssm_w=ssm_w),
        grid=(tokens // tm,),
        in_specs=[pl.BlockSpec((tm, ssm_w), lambda i: (i, 0)),
                  pl.BlockSpec((tm, attn_w), lambda i: (i, 0)),
                  pl.BlockSpec((tm, attn_w), lambda i: (i, 0)),
                  pl.BlockSpec((1, attn_w), lambda i: (0, 0)),
                  pl.BlockSpec((ssm_w + attn_w, d_model), lambda i: (0, 0)),
                  pl.BlockSpec((tm, d_model), lambda i: (i, 0)),
                  pl.BlockSpec((1, d_model), lambda i: (0, 0))],
        out_specs=pl.BlockSpec((tm, d_model), lambda i: (i, 0)),
        out_shape=jax.ShapeDtypeStruct((tokens, d_model), F32),
        compiler_params=_cparams("arbitrary"),
        name="out_proj",
    )(ys, oa.reshape(tokens, attn_w), za, row(g_attn), w_out.astype(BF16), x2, row(norm_post_g))
    return out.reshape(bsz, seq, d_model)


def kernel(x, norm_pre_g, w_in, b_forget, ssm_a_re, ssm_a_im, ssm_log_dt, ssm_b_re, ssm_b_im,
           ssm_c_re, ssm_c_im, ssm_d, w_glu, b_glu, g_ssm, g_attn, w_out, norm_post_g):
    h = x
    for layer in range(norm_pre_g.shape[0]):
        h = _layer(h, norm_pre_g[layer], w_in[layer], b_forget[layer], ssm_a_re[layer],
                   ssm_a_im[layer], ssm_log_dt[layer], ssm_b_re[layer], ssm_b_im[layer],
                   ssm_c_re[layer], ssm_c_im[layer], ssm_d[layer], w_glu[layer], b_glu[layer],
                   g_ssm[layer], g_attn[layer], w_out[layer], norm_post_g[layer])
    return h
```

```python
import functools
import math

import jax
import jax.numpy as jnp
from jax import lax
from jax.experimental import pallas as pl
from jax.experimental.pallas import tpu as pltpu

F32 = jnp.float32
BF16 = jnp.bfloat16

SSM_GROUP_CH = 16
SSM_STATE = 64
HEAD_DIM = 64
RMS_EPS = 1e-6

LANES = 128
SUBLANES = 8
VMEM_LIMIT_BYTES = 56 * 1024 * 1024

TM_PROJ = 512
TL_SSM = 256
TQ_ATTN = 1024
TK_ATTN = 256
QCOL_BLOCK = 512
CUM_CHUNK = 512
PV_ONES_ROWS = 2 * SUBLANES
LOG2E = math.log2(math.e)
NT_DIMS = (((1,), (1,)), ((), ()))
SG_GROUPS = LANES // SSM_GROUP_CH
SG_STATE = SG_GROUPS * SSM_STATE
SCAN_W = 256
NEG_BIG = -1e30


def _cparams(*sem):
    return pltpu.CompilerParams(dimension_semantics=sem, vmem_limit_bytes=VMEM_LIMIT_BYTES)


def _rms(v, g):
    return v * lax.rsqrt(jnp.mean(v * v, axis=-1, keepdims=True) + RMS_EPS) * g


def _in_proj_kernel(x_ref, g_ref, w_ref, wt_ref, uz_ref, k_ref, za_ref, qt_ref, vt_ref, ft_ref,
                    *, ssm_w, attn_w, n_heads):
    xb = _rms(x_ref[...], g_ref[...]).astype(BF16)
    c0 = 2 * ssm_w
    uz_ref[...] = jnp.dot(xb, w_ref[:, :c0], preferred_element_type=F32).astype(BF16)
    k_ref[...] = jnp.dot(xb, w_ref[:, c0:c0 + attn_w], preferred_element_type=F32).astype(BF16)
    za_ref[...] = jnp.dot(xb, w_ref[:, c0 + attn_w:], preferred_element_type=F32).astype(BF16)
    qt = lax.dot_general(wt_ref[:attn_w], xb, NT_DIMS, preferred_element_type=F32)
    qt_ref[0] = (qt * (HEAD_DIM ** -0.5 * LOG2E)).astype(BF16)
    vt_ref[0] = lax.dot_general(wt_ref[attn_w:2 * attn_w], xb, NT_DIMS,
                                preferred_element_type=F32).astype(BF16)
    ft = lax.dot_general(wt_ref[2 * attn_w:], xb, NT_DIMS, preferred_element_type=F32)
    ft_ref[0] = ft[:n_heads]


def _forget_cum_kernel(ft_ref, bf_ref, aug_ref, *, seq, n_hp, chunk):
    f = ft_ref[0] + bf_ref[...]
    lf = jnp.minimum(f, 0.0) - jnp.log1p(jnp.exp(-jnp.abs(f)))
    lane = lax.broadcasted_iota(jnp.int32, lf.shape, 1)
    k = 1
    while k < seq:
        lf = lf + jnp.where(lane >= k, pltpu.roll(lf, k, axis=1), 0.0)
        k *= 2
    nb = lf * (-LOG2E)
    hi = nb.astype(BF16).astype(F32)
    mid = (nb - hi).astype(BF16).astype(F32)
    lo = (nb - hi - mid).astype(BF16).astype(F32)
    parts = (hi, mid, lo)
    rid = lax.broadcasted_iota(jnp.int32, (SUBLANES, seq), 0)
    eye = (lax.broadcasted_iota(jnp.int32, (chunk, chunk), 0)
           == lax.broadcasted_iota(jnp.int32, (chunk, chunk), 1)).astype(BF16)
    pad = jnp.zeros((LANES - SUBLANES, seq), F32)
    for p in range(n_hp):
        rows = jnp.zeros((SUBLANES, seq), F32)
        for j in range(LANES // HEAD_DIM):
            for t, part in enumerate(parts):
                src = part[2 * p + j:2 * p + j + 1, :]
                rows = jnp.where(rid == 3 * j + t, jnp.broadcast_to(src, rows.shape), rows)
        bmat = jnp.concatenate([rows, pad], axis=0).astype(BF16)
        for c in range(seq // chunk):
            blk = lax.dot_general(eye, bmat[:, c * chunk:(c + 1) * chunk], NT_DIMS,
                                  preferred_element_type=F32)
            aug_ref[0, p, c * chunk:(c + 1) * chunk, :] = blk.astype(BF16)


def _cmul(ar, ai, xr, xi):
    return ar * xr - ai * xi, ar * xi + ai * xr


def _s5_kernel(uz_ref, bsg_ref, csg_ref, coef_ref, d_ref, wglu_ref, bglu_ref, gs_ref,
               ys_ref, bu_ref, hb_ref, carry_ref, *, ssm_w, tl):
    n_sg = ssm_w // LANES
    sg_cols = 2 * SG_STATE
    nj = tl // SUBLANES
    log_nj = nj.bit_length() - 1

    @pl.when(pl.program_id(1) == 0)
    def _():
        carry_ref[...] = jnp.zeros_like(carry_ref)

    ri = lax.broadcasted_iota(jnp.int32, (tl, tl), 0)
    ci = lax.broadcasted_iota(jnp.int32, (tl, tl), 1)
    log_sub = SUBLANES.bit_length() - 1
    perm = (ci == ((ri & (SUBLANES - 1)) << log_nj) + (ri >> log_sub)).astype(BF16)
    unperm = (ri == ((ci & (SUBLANES - 1)) << log_nj) + (ci >> log_sub)).astype(BF16)
    uzp = jnp.dot(perm, uz_ref[...], preferred_element_type=F32).astype(BF16)

    rid = lax.broadcasted_iota(jnp.int32, (SUBLANES, SG_STATE), 0)
    for i in range(n_sg):
        re = slice(i * sg_cols, i * sg_cols + SG_STATE)
        im = slice(i * sg_cols + SG_STATE, (i + 1) * sg_cols)
        cs = slice(i * SG_STATE, (i + 1) * SG_STATE)
        bu_ref[:, i * sg_cols:(i + 1) * sg_cols] = jnp.dot(
            uzp[:, i * LANES:(i + 1) * LANES], bsg_ref[i], preferred_element_type=F32)
        ar, ai = coef_ref[8, :, cs], coef_ref[9, :, cs]

        hr = jnp.zeros((SUBLANES, SG_STATE), F32)
        hi = jnp.zeros((SUBLANES, SG_STATE), F32)
        for j in range(nj):
            rows = slice(j * SUBLANES, (j + 1) * SUBLANES)
            tr, ti = _cmul(ar, ai, hr, hi)
            hr, hi = tr + bu_ref[rows, re], ti + bu_ref[rows, im]
            bu_ref[rows, re] = hr
            bu_ref[rows, im] = hi

        for t in range(3):
            k = 1 << t
            tr, ti = _cmul(coef_ref[2 * t, :, cs], coef_ref[2 * t + 1, :, cs],
                           pltpu.roll(hr, k, axis=0), pltpu.roll(hi, k, axis=0))
            hr, hi = hr + tr, hi + ti
        cr = jnp.broadcast_to(carry_ref[0:1, cs], hr.shape)
        cim = jnp.broadcast_to(carry_ref[1:2, cs], hi.shape)
        tr, ti = _cmul(coef_ref[6, :, cs], coef_ref[7, :, cs], cr, cim)
        hr, hi = hr + tr, hi + ti
        carry_ref[0:1, cs] = hr[SUBLANES - 1:SUBLANES, :]
        carry_ref[1:2, cs] = hi[SUBLANES - 1:SUBLANES, :]
        gr = jnp.where(rid == 0, cr, pltpu.roll(hr, 1, axis=0))
        gi = jnp.where(rid == 0, cim, pltpu.roll(hi, 1, axis=0))

        for j in range(0, nj, 2):
            outs = []
            for jj in (j, j + 1):
                rows = slice(jj * SUBLANES, (jj + 1) * SUBLANES)
                gr, gi = _cmul(ar, ai, gr, gi)
                outs.append((bu_ref[rows, re] + gr, bu_ref[rows, im] + gi))
            rows2 = slice(j * SUBLANES, (j + 2) * SUBLANES)
            hb_ref[rows2, re] = jnp.concatenate([outs[0][0], outs[1][0]], axis=0).astype(BF16)
            hb_ref[rows2, im] = jnp.concatenate([outs[0][1], outs[1][1]], axis=0).astype(BF16)

    y = jnp.concatenate(
        [jnp.dot(hb_ref[:, i * sg_cols:(i + 1) * sg_cols], csg_ref[i],
                 preferred_element_type=F32) for i in range(n_sg)], axis=-1)
    u = uzp[:, :ssm_w].astype(F32)
    z = uzp[:, ssm_w:].astype(F32)
    y = jax.nn.gelu(y + d_ref[...] * u)
    gate = jnp.dot(y.astype(BF16), wglu_ref[...], preferred_element_type=F32) + bglu_ref[...]
    y = y * jax.nn.sigmoid(gate)
    y = y * (z * jax.nn.sigmoid(z))
    ysp = _rms(y, gs_ref[...]).astype(BF16)
    ys_ref[...] = jnp.dot(unperm, ysp, preferred_element_type=F32).astype(BF16)


def _attn_kernel(qt_ref, k_ref, aug_ref, vt_ref, o_ref, qc_ref, sa_ref, sb_ref, sc_ref, sd_ref,
                 ta_ref, tb_ref, tc_ref, td_ref, m_ref, acc_ref, *, tq, tk):
    qi = pl.program_id(2)
    n_h = LANES // HEAD_DIM

    zero = jnp.zeros((HEAD_DIM, tq), BF16)
    for h in range(n_h):
        for hh in range(n_h):
            qc_ref[h * HEAD_DIM:(h + 1) * HEAD_DIM, hh * tq:(hh + 1) * tq] = (
                qt_ref[0, h * HEAD_DIM:(h + 1) * HEAD_DIM, :] if h == hh else zero)
    r = lax.broadcasted_iota(jnp.int32, (LANES, n_h * tq), 0)
    c = lax.broadcasted_iota(jnp.int32, (LANES, n_h * tq), 1)
    ones = (r >= 0) & (r < 3) & (c < tq)
    for h in range(1, n_h):
        ones = ones | ((r >= 3 * h) & (r < 3 * h + 3) & (c >= h * tq) & (c < (h + 1) * tq))
    qc_ref[LANES:, :] = jnp.where(ones, 1.0, 0.0).astype(BF16)

    ones_rows = jnp.ones((PV_ONES_ROWS, tk), BF16)

    def pv(pb, start, h):
        lhs = jnp.concatenate([vt_ref[0, h * HEAD_DIM:(h + 1) * HEAD_DIM, pl.ds(start, tk)],
                               ones_rows], axis=0)
        return jnp.dot(lhs, pb, preferred_element_type=F32)

    n_pairs = qi * (tq // (2 * tk))

    def key_rows(start):
        return jnp.concatenate([k_ref[0, pl.ds(start, tk), :],
                                aug_ref[0, 0, pl.ds(start, tk), :]], axis=1)

    def scores(lhs, cols, s_ref, t_ref):
        s = jnp.dot(lhs, qc_ref[:, cols], preferred_element_type=F32)
        s_ref[:, cols] = s
        t_ref[:, cols] = jnp.max(s, axis=0, keepdims=True)

    def softmax_pv(start, cols, s_ref, t_ref):
        h = cols.start // tq
        qcols = slice(cols.start - h * tq, cols.stop - h * tq)
        m_old = m_ref[:, cols]
        m_new = jnp.maximum(m_old, t_ref[:, cols])
        alpha = jnp.exp2(m_old - m_new)
        pb = jnp.exp2(s_ref[:, cols] - m_new).astype(BF16)
        m_ref[:, cols] = m_new
        acc_ref[h, :, qcols] = alpha * acc_ref[h, :, qcols] + pv(pb, start, h)

    sets = (((sa_ref, ta_ref), (sb_ref, tb_ref)), ((sc_ref, tc_ref), (sd_ref, td_ref)))
    col_blocks = [slice(c, c + QCOL_BLOCK) for c in range(0, n_h * tq, QCOL_BLOCK)]

    def pair_body(p, cur, nxt):
        base = pl.multiple_of(p * (2 * tk), 2 * tk)
        for t in range(2):
            start = base + t * tk
            if nxt is not None:
                lhs = key_rows(start + 2 * tk)
            for cols in col_blocks:
                if nxt is not None:
                    scores(lhs, cols, *nxt[t])
                softmax_pv(start, cols, *cur[t])

    n_sub = tq // tk
    d0 = pl.multiple_of(qi * tq, tq)
    kk = lax.broadcasted_iota(jnp.int32, (tk, tk), 0)
    qq = lax.broadcasted_iota(jnp.int32, (tk, tk), 1)
    causal = jnp.where(kk <= qq, 0.0, NEG_BIG)
    blocks = [(h, c) for c in range(n_sub) for h in range(n_h)]
    sq = {}
    for r in range(n_sub):
        lhs = key_rows(d0 + r * tk)
        for h, c in blocks:
            if c >= r:
                cols = slice(h * tq + c * tk, h * tq + (c + 1) * tk)
                s = jnp.dot(lhs, qc_ref[:, cols], preferred_element_type=F32)
                sq[h, c, r] = s + causal if r == c else s
    first_pair = [(t, cols) for t in range(2) for cols in col_blocks]
    first_lhs = [key_rows(t * tk) for t in range(2)]
    state = {}
    n_done = 0
    for r in range(n_sub):
        for h, c in blocks:
            if c < r:
                continue
            s = sq[h, c, r]
            t = jnp.max(s, axis=0, keepdims=True)
            if r == 0:
                m_new = t
            else:
                m_old, acc = state[h, c]
                m_new = jnp.maximum(m_old, t)
                alpha = jnp.exp2(m_old - m_new)
            upd = pv(jnp.exp2(s - m_new).astype(BF16), d0 + r * tk, h)
            acc = upd if r == 0 else alpha * acc + upd
            state[h, c] = (m_new, acc)
            if r == c:
                m_ref[:, h * tq + c * tk:h * tq + (c + 1) * tk] = m_new
                acc_ref[h, :, c * tk:(c + 1) * tk] = acc
            n_done += 1
            if n_done % 2 == 0 and first_pair:
                t0, cols0 = first_pair.pop(0)
                scores(first_lhs[t0], cols0, *sets[0][t0])
    for t0, cols0 in first_pair:
        scores(first_lhs[t0], cols0, *sets[0][t0])

    def pair(p, carry):
        for parity in range(2):
            @pl.when(p % 2 == parity)
            def _(parity=parity):
                pair_body(p, sets[parity], sets[1 - parity])
        return carry

    lax.fori_loop(0, n_pairs - 1, pair, 0)

    for parity in range(2):
        @pl.when((n_pairs > 0) & ((n_pairs - 1) % 2 == parity))
        def _(parity=parity):
            pair_body(n_pairs - 1, sets[parity], None)

    ot = jnp.concatenate(
        [acc_ref[h, :HEAD_DIM, :] * (1.0 / acc_ref[h, HEAD_DIM:HEAD_DIM + 1, :])
         for h in range(n_h)], axis=0)
    o_ref[0] = ot.T


def _out_proj_kernel(ys_ref, oa_ref, za_ref, ga_ref, w_ref, x_ref, gp_ref, out_ref, *, ssm_w):
    z = za_ref[...].astype(F32)
    ya = _rms(oa_ref[...] * (z * jax.nn.sigmoid(z)), ga_ref[...]).astype(BF16)
    y = (jnp.dot(ys_ref[...], w_ref[:ssm_w, :], preferred_element_type=F32)
         + jnp.dot(ya, w_ref[ssm_w:, :], preferred_element_type=F32))
    out_ref[...] = x_ref[...] + _rms(y, gp_ref[...])


def _s5_params(a_re, a_im, log_dt, b_re, b_im, c_re, c_im, nj):
    g, n = a_re.shape
    h = b_re.shape[-1]
    n_sg = g // SG_GROUPS
    dt = jnp.exp(log_dt)[:, None]
    mag, ang = jnp.exp(dt * a_re), dt * a_im
    abar_re, abar_im = mag * jnp.cos(ang), mag * jnp.sin(ang)
    den = a_re * a_re + a_im * a_im
    nr, ni = abar_re - 1.0, abar_im
    coef_re = (nr * a_re + ni * a_im) / den
    coef_im = (ni * a_re - nr * a_im) / den
    bbar_re = coef_re[..., None] * b_re - coef_im[..., None] * b_im
    bbar_im = coef_re[..., None] * b_im + coef_im[..., None] * b_re
    eye = jnp.eye(SG_GROUPS, dtype=F32)
    bb = jnp.stack([bbar_re, bbar_im]).reshape(2, n_sg, SG_GROUPS, n, h)
    bsg = jnp.einsum('pignh,gk->ighpkn', bb, eye).reshape(n_sg, SG_GROUPS * h, 2 * SG_GROUPS * n)
    cc = jnp.stack([c_re, -c_im]).reshape(2, n_sg, SG_GROUPS, h, n)
    csg = jnp.einsum('pighn,gk->ipgnkh', cc, eye).reshape(n_sg, 2 * SG_GROUPS * n, SG_GROUPS * h)

    def apow(m):
        m = jnp.asarray(m, F32)[:, None, None]
        pm, pa = jnp.exp(m * (dt * a_re)), m * ang
        return (pm * jnp.cos(pa)).reshape(-1, g * n), (pm * jnp.sin(pa)).reshape(-1, g * n)

    rows = jnp.arange(SUBLANES)
    coef = []
    for k in (1, 2, 4):
        pr, pi = apow([nj * k])
        live = (rows >= k)[:, None]
        coef += [jnp.where(live, pr, 0.0), jnp.where(live, pi, 0.0)]
    pr, pi = apow(nj * (rows + 1))
    coef += [pr, pi]
    pr, pi = apow(jnp.ones((SUBLANES,)))
    coef += [pr, pi]
    return bsg.astype(BF16), csg.astype(BF16), jnp.stack(coef)


def _layer(x, norm_pre_g, w_in, b_forget, a_re, a_im, log_dt, b_re, b_im, c_re, c_im, d_skip,
           w_glu, b_glu, g_ssm, g_attn, w_out, norm_post_g):
    bsz, seq, d_model = x.shape
    n_heads = b_forget.shape[0]
    attn_w = n_heads * HEAD_DIM
    ssm_w = d_skip.shape[0]
    tokens = bsz * seq
    tm, tl, tq = TM_PROJ, TL_SSM, TQ_ATTN
    assert seq % tm == 0 and seq % tl == 0 and seq % tq == 0
    assert ssm_w % LANES == 0 and attn_w % LANES == 0 and n_heads <= 2 * SUBLANES
    nt = seq // tm
    x2 = x.reshape(tokens, d_model)
    row = lambda v: v.reshape(1, -1).astype(F32)

    c0 = 2 * ssm_w
    w_q, w_k, w_v, w_za, w_f = (w_in[:, c0 + j * attn_w:c0 + (j + 1) * attn_w] for j in range(5))
    w_main = jnp.concatenate([w_in[:, :c0], w_k, w_za], axis=1).astype(BF16)
    w_t = jnp.concatenate([w_q.T, w_v.T, w_f.T,
                           jnp.zeros((2 * SUBLANES - n_heads, d_model), F32)], axis=0).astype(BF16)

    uz, k, za, qt, vt, ft = pl.pallas_call(
        functools.partial(_in_proj_kernel, ssm_w=ssm_w, attn_w=attn_w, n_heads=n_heads),
        grid=(tokens // tm,),
        in_specs=[pl.BlockSpec((tm, d_model), lambda i: (i, 0)),
                  pl.BlockSpec((1, d_model), lambda i: (0, 0)),
                  pl.BlockSpec(w_main.shape, lambda i: (0, 0)),
                  pl.BlockSpec(w_t.shape, lambda i: (0, 0))],
        out_specs=[pl.BlockSpec((tm, 2 * ssm_w), lambda i: (i, 0)),
                   pl.BlockSpec((tm, attn_w), lambda i: (i, 0)),
                   pl.BlockSpec((tm, attn_w), lambda i: (i, 0)),
                   pl.BlockSpec((1, attn_w, tm), lambda i: (i // nt, 0, i % nt)),
                   pl.BlockSpec((1, attn_w, tm), lambda i: (i // nt, 0, i % nt)),
                   pl.BlockSpec((1, n_heads, tm), lambda i: (i // nt, 0, i % nt))],
        out_shape=[jax.ShapeDtypeStruct((tokens, 2 * ssm_w), BF16),
                   jax.ShapeDtypeStruct((tokens, attn_w), BF16),
                   jax.ShapeDtypeStruct((tokens, attn_w), BF16),
                   jax.ShapeDtypeStruct((bsz, attn_w, seq), BF16),
                   jax.ShapeDtypeStruct((bsz, attn_w, seq), BF16),
                   jax.ShapeDtypeStruct((bsz, n_heads, seq), F32)],
        compiler_params=_cparams("arbitrary"),
        name="in_proj",
    )(x2, row(norm_pre_g), w_main, w_t)

    n_hp = n_heads // (LANES // HEAD_DIM)
    aug = pl.pallas_call(
        functools.partial(_forget_cum_kernel, seq=seq, n_hp=n_hp, chunk=CUM_CHUNK),
        grid=(bsz,),
        in_specs=[pl.BlockSpec((1, n_heads, seq), lambda b: (b, 0, 0)),
                  pl.BlockSpec((n_heads, 1), lambda b: (0, 0))],
        out_specs=pl.BlockSpec((1, n_hp, seq, LANES), lambda b: (b, 0, 0, 0)),
        out_shape=jax.ShapeDtypeStruct((bsz, n_hp, seq, LANES), BF16),
        compiler_params=_cparams("arbitrary"),
        name="forget_cum",
    )(ft, b_forget.reshape(n_heads, 1).astype(F32))

    assert tl % (2 * SUBLANES * SUBLANES) == 0 and (tl // SUBLANES) & (tl // SUBLANES - 1) == 0
    bsg, csg, coef = _s5_params(a_re, a_im, log_dt, b_re, b_im, c_re, c_im, tl // SUBLANES)
    n_sg = ssm_w // LANES
    n_state = a_re.size
    nl = seq // tl
    ys = pl.pallas_call(
        functools.partial(_s5_kernel, ssm_w=ssm_w, tl=tl),
        grid=(bsz, nl),
        in_specs=[pl.BlockSpec((tl, 2 * ssm_w), lambda b, t: (b * nl + t, 0)),
                  pl.BlockSpec(bsg.shape, lambda b, t: (0, 0, 0)),
                  pl.BlockSpec(csg.shape, lambda b, t: (0, 0, 0)),
                  pl.BlockSpec(coef.shape, lambda b, t: (0, 0, 0)),
                  pl.BlockSpec((1, ssm_w), lambda b, t: (0, 0)),
                  pl.BlockSpec((ssm_w, ssm_w), lambda b, t: (0, 0)),
                  pl.BlockSpec((1, ssm_w), lambda b, t: (0, 0)),
                  pl.BlockSpec((1, ssm_w), lambda b, t: (0, 0))],
        out_specs=pl.BlockSpec((tl, ssm_w), lambda b, t: (b * nl + t, 0)),
        out_shape=jax.ShapeDtypeStruct((tokens, ssm_w), BF16),
        scratch_shapes=[pltpu.VMEM((tl, 2 * n_state), F32),
                        pltpu.VMEM((tl, 2 * n_state), BF16),
                        pltpu.VMEM((2, n_state), F32)],
        compiler_params=_cparams("arbitrary", "arbitrary"),
        name="s5",
    )(uz, bsg, csg, coef, row(d_skip), w_glu.astype(BF16), row(b_glu), row(g_ssm))

    tk = TK_ATTN
    assert tq % (2 * tk) == 0 and 3 * (LANES // HEAD_DIM) <= LANES
    nq = seq // tq
    wq = (LANES // HEAD_DIM) * tq
    oa = pl.pallas_call(
        functools.partial(_attn_kernel, tq=tq, tk=tk),
        grid=(bsz, n_hp, nq),
        in_specs=[pl.BlockSpec((1, LANES, tq), lambda b, p, i: (b, p, i)),
                  pl.BlockSpec((1, seq, LANES), lambda b, p, i: (b, 0, p)),
                  pl.BlockSpec((1, 1, seq, LANES), lambda b, p, i: (b, p, 0, 0)),
                  pl.BlockSpec((1, LANES, seq), lambda b, p, i: (b, p, 0))],
        out_specs=pl.BlockSpec((1, tq, LANES), lambda b, p, i: (b, i, p)),
        out_shape=jax.ShapeDtypeStruct((bsz, seq, attn_w), F32),
        scratch_shapes=[pltpu.VMEM((2 * LANES, wq), BF16),
                        pltpu.VMEM((tk, wq), F32),
                        pltpu.VMEM((tk, wq), F32),
                        pltpu.VMEM((tk, wq), F32),
                        pltpu.VMEM((tk, wq), F32),
                        pltpu.VMEM((1, wq), F32),
                        pltpu.VMEM((1, wq), F32),
                        pltpu.VMEM((1, wq), F32),
                        pltpu.VMEM((1, wq), F32),
                        pltpu.VMEM((1, wq), F32),
                        pltpu.VMEM((LANES // HEAD_DIM, HEAD_DIM + PV_ONES_ROWS, tq), F32)],
        compiler_params=_cparams("arbitrary", "arbitrary", "arbitrary"),
        name="fox_attn",
    )(qt, k.reshape(bsz, seq, attn_w), aug, vt)

    out = pl.pallas_call(
        functools.partial(_out_proj_kernel, ssm_w=ssm_w),
        grid=(tokens // tm,),
        in_specs=[pl.BlockSpec((tm, ssm_w), lambda i: (i, 0)),
                  pl.BlockSpec((tm, attn_w), lambda i: (i, 0)),
                  pl.BlockSpec((tm, attn_w), lambda i: (i, 0)),
                  pl.BlockSpec((1, attn_w), lambda i: (0, 0)),
                  pl.BlockSpec((ssm_w + attn_w, d_model), lambda i: (0, 0)),
                  pl.BlockSpec((tm, d_model), lambda i: (i, 0)),
                  pl.BlockSpec((1, d_model), lambda i: (0, 0))],
        out_specs=pl.BlockSpec((tm, d_model), lambda i: (i, 0)),
        out_shape=jax.ShapeDtypeStruct((tokens, d_model), F32),
        compiler_params=_cparams("arbitrary"),
        name="out_proj",
    )(ys, oa.reshape(tokens, attn_w), za, row(g_attn), w_out.astype(BF16), x2, row(norm_post_g))
    return out.reshape(bsz, seq, d_model)


def kernel(x, norm_pre_g, w_in, b_forget, ssm_a_re, ssm_a_im, ssm_log_dt, ssm_b_re, ssm_b_im,
           ssm_c_re, ssm_c_im, ssm_d, w_glu, b_glu, g_ssm, g_attn, w_out, norm_post_g):
    h = x
    for layer in range(norm_pre_g.shape[0]):
        h = _layer(h, norm_pre_g[layer], w_in[layer], b_forget[layer], ssm_a_re[layer],
                   ssm_a_im[layer], ssm_log_dt[layer], ssm_b_re[layer], ssm_b_im[layer],
                   ssm_c_re[layer], ssm_c_im[layer], ssm_d[layer], w_glu[layer], b_glu[layer],
                   g_ssm[layer], g_attn[layer], w_out[layer], norm_post_g[layer])
    return h
```

```python
import functools
import math

import jax
import jax.numpy as jnp
from jax import lax
from jax.experimental import pallas as pl
from jax.experimental.pallas import tpu as pltpu

F32 = jnp.float32
BF16 = jnp.bfloat16

SSM_GROUP_CH = 16
SSM_STATE = 64
HEAD_DIM = 64
RMS_EPS = 1e-6

LANES = 128
SUBLANES = 8
VMEM_LIMIT_BYTES = 56 * 1024 * 1024

TM_PROJ = 512
TL_SSM = 256
TQ_ATTN = 1024
TK_ATTN = 256
QCOL_BLOCK = 512
CUM_CHUNK = 128
PV_ONES_ROWS = 2 * SUBLANES
LOG2E = math.log2(math.e)
NT_DIMS = (((1,), (1,)), ((), ()))
SG_GROUPS = LANES // SSM_GROUP_CH
SG_STATE = SG_GROUPS * SSM_STATE
SCAN_W = 256
NEG_BIG = -1e30


def _cparams(*sem):
    return pltpu.CompilerParams(dimension_semantics=sem, vmem_limit_bytes=VMEM_LIMIT_BYTES)


def _rms(v, g):
    return v * lax.rsqrt(jnp.mean(v * v, axis=-1, keepdims=True) + RMS_EPS) * g


def _in_proj_kernel(x_ref, g_ref, w_ref, wt_ref, uz_ref, k_ref, za_ref, qt_ref, vt_ref, ft_ref,
                    *, ssm_w, attn_w, n_heads):
    xb = _rms(x_ref[...], g_ref[...]).astype(BF16)
    c0 = 2 * ssm_w
    uz_ref[...] = jnp.dot(xb, w_ref[:, :c0], preferred_element_type=F32).astype(BF16)
    k_ref[...] = jnp.dot(xb, w_ref[:, c0:c0 + attn_w], preferred_element_type=F32).astype(BF16)
    za_ref[...] = jnp.dot(xb, w_ref[:, c0 + attn_w:], preferred_element_type=F32).astype(BF16)
    qt = lax.dot_general(wt_ref[:attn_w], xb, NT_DIMS, preferred_element_type=F32)
    qt_ref[0] = (qt * (HEAD_DIM ** -0.5 * LOG2E)).astype(BF16)
    vt_ref[0] = lax.dot_general(wt_ref[attn_w:2 * attn_w], xb, NT_DIMS,
                                preferred_element_type=F32).astype(BF16)
    ft = lax.dot_general(wt_ref[2 * attn_w:], xb, NT_DIMS, preferred_element_type=F32)
    ft_ref[0] = ft[:n_heads]


def _forget_cum_kernel(ft_ref, bf_ref, aug_ref, *, seq, n_hp, chunk):
    f = ft_ref[0] + bf_ref[...]
    lf = jnp.minimum(f, 0.0) - jnp.log1p(jnp.exp(-jnp.abs(f)))
    lane = lax.broadcasted_iota(jnp.int32, lf.shape, 1)
    k = 1
    while k < seq:
        lf = lf + jnp.where(lane >= k, pltpu.roll(lf, k, axis=1), 0.0)
        k *= 2
    nb = lf * (-LOG2E)
    hi = nb.astype(BF16).astype(F32)
    mid = (nb - hi).astype(BF16).astype(F32)
    lo = (nb - hi - mid).astype(BF16).astype(F32)
    parts = (hi, mid, lo)
    rid = lax.broadcasted_iota(jnp.int32, (SUBLANES, seq), 0)
    eye = (lax.broadcasted_iota(jnp.int32, (chunk, chunk), 0)
           == lax.broadcasted_iota(jnp.int32, (chunk, chunk), 1)).astype(BF16)
    pad = jnp.zeros((LANES - SUBLANES, seq), F32)
    for p in range(n_hp):
        rows = jnp.zeros((SUBLANES, seq), F32)
        for j in range(LANES // HEAD_DIM):
            for t, part in enumerate(parts):
                src = part[2 * p + j:2 * p + j + 1, :]
                rows = jnp.where(rid == 3 * j + t, jnp.broadcast_to(src, rows.shape), rows)
        bmat = jnp.concatenate([rows, pad], axis=0).astype(BF16)
        for c in range(seq // chunk):
            blk = lax.dot_general(eye, bmat[:, c * chunk:(c + 1) * chunk], NT_DIMS,
                                  preferred_element_type=F32)
            aug_ref[0, p, c * chunk:(c + 1) * chunk, :] = blk.astype(BF16)


def _cmul(ar, ai, xr, xi):
    return ar * xr - ai * xi, ar * xi + ai * xr


def _s5_kernel(uz_ref, perm_ref, bsg_ref, csg_ref, coef_ref, d_ref, wglu_ref, bglu_ref, gs_ref,
               ys_ref, bu_ref, hb_ref, carry_ref, *, ssm_w, tl):
    n_sg = ssm_w // LANES
    sg_cols = 2 * SG_STATE
    nj = tl // SUBLANES

    @pl.when(pl.program_id(1) == 0)
    def _():
        carry_ref[...] = jnp.zeros_like(carry_ref)

    uzp = jnp.dot(perm_ref[0], uz_ref[...], preferred_element_type=F32).astype(BF16)

    rid = lax.broadcasted_iota(jnp.int32, (SUBLANES, SG_STATE), 0)
    for i in range(n_sg):
        re = slice(i * sg_cols, i * sg_cols + SG_STATE)
        im = slice(i * sg_cols + SG_STATE, (i + 1) * sg_cols)
        cs = slice(i * SG_STATE, (i + 1) * SG_STATE)
        bu_ref[:, i * sg_cols:(i + 1) * sg_cols] = jnp.dot(
            uzp[:, i * LANES:(i + 1) * LANES], bsg_ref[i], preferred_element_type=F32)
        ar, ai = coef_ref[8, :, cs], coef_ref[9, :, cs]

        hr = jnp.zeros((SUBLANES, SG_STATE), F32)
        hi = jnp.zeros((SUBLANES, SG_STATE), F32)
        for j in range(nj):
            rows = slice(j * SUBLANES, (j + 1) * SUBLANES)
            tr, ti = _cmul(ar, ai, hr, hi)
            hr, hi = tr + bu_ref[rows, re], ti + bu_ref[rows, im]
            bu_ref[rows, re] = hr
            bu_ref[rows, im] = hi

        for t in range(3):
            k = 1 << t
            tr, ti = _cmul(coef_ref[2 * t, :, cs], coef_ref[2 * t + 1, :, cs],
                           pltpu.roll(hr, k, axis=0), pltpu.roll(hi, k, axis=0))
            hr, hi = hr + tr, hi + ti
        cr = jnp.broadcast_to(carry_ref[0:1, cs], hr.shape)
        cim = jnp.broadcast_to(carry_ref[1:2, cs], hi.shape)
        tr, ti = _cmul(coef_ref[6, :, cs], coef_ref[7, :, cs], cr, cim)
        hr, hi = hr + tr, hi + ti
        carry_ref[0:1, cs] = hr[SUBLANES - 1:SUBLANES, :]
        carry_ref[1:2, cs] = hi[SUBLANES - 1:SUBLANES, :]
        gr = jnp.where(rid == 0, cr, pltpu.roll(hr, 1, axis=0))
        gi = jnp.where(rid == 0, cim, pltpu.roll(hi, 1, axis=0))

        for j in range(0, nj, 2):
            outs = []
            for jj in (j, j + 1):
                rows = slice(jj * SUBLANES, (jj + 1) * SUBLANES)
                gr, gi = _cmul(ar, ai, gr, gi)
                outs.append((bu_ref[rows, re] + gr, bu_ref[rows, im] + gi))
            rows2 = slice(j * SUBLANES, (j + 2) * SUBLANES)
            hb_ref[rows2, re] = jnp.concatenate([outs[0][0], outs[1][0]], axis=0).astype(BF16)
            hb_ref[rows2, im] = jnp.concatenate([outs[0][1], outs[1][1]], axis=0).astype(BF16)

    y = jnp.concatenate(
        [jnp.dot(hb_ref[:, i * sg_cols:(i + 1) * sg_cols], csg_ref[i],
                 preferred_element_type=F32) for i in range(n_sg)], axis=-1)
    u = uzp[:, :ssm_w].astype(F32)
    z = uzp[:, ssm_w:].astype(F32)
    y = jax.nn.gelu(y + d_ref[...] * u)
    gate = jnp.dot(y.astype(BF16), wglu_ref[...], preferred_element_type=F32) + bglu_ref[...]
    y = y * jax.nn.sigmoid(gate)
    y = y * (z * jax.nn.sigmoid(z))
    ysp = _rms(y, gs_ref[...]).astype(BF16)
    ys_ref[...] = jnp.dot(perm_ref[1], ysp, preferred_element_type=F32).astype(BF16)


def _attn_kernel(qt_ref, k_ref, aug_ref, vt_ref, o_ref, qc_ref, sa_ref, sb_ref, sc_ref, sd_ref,
                 ta_ref, tb_ref, tc_ref, td_ref, m_ref, acc_ref, *, tq, tk):
    qi = pl.program_id(2)
    n_h = LANES // HEAD_DIM

    zero = jnp.zeros((HEAD_DIM, tq), BF16)
    for h in range(n_h):
        for hh in range(n_h):
            qc_ref[h * HEAD_DIM:(h + 1) * HEAD_DIM, hh * tq:(hh + 1) * tq] = (
                qt_ref[0, h * HEAD_DIM:(h + 1) * HEAD_DIM, :] if h == hh else zero)
    r = lax.broadcasted_iota(jnp.int32, (LANES, n_h * tq), 0)
    c = lax.broadcasted_iota(jnp.int32, (LANES, n_h * tq), 1)
    ones = (r >= 0) & (r < 3) & (c < tq)
    for h in range(1, n_h):
        ones = ones | ((r >= 3 * h) & (r < 3 * h + 3) & (c >= h * tq) & (c < (h + 1) * tq))
    qc_ref[LANES:, :] = jnp.where(ones, 1.0, 0.0).astype(BF16)

    ones_rows = jnp.ones((PV_ONES_ROWS, tk), BF16)

    def pv(pb, start, h):
        lhs = jnp.concatenate([vt_ref[0, h * HEAD_DIM:(h + 1) * HEAD_DIM, pl.ds(start, tk)],
                               ones_rows], axis=0)
        return jnp.dot(lhs, pb, preferred_element_type=F32)

    n_pairs = qi * (tq // (2 * tk))

    def key_rows(start):
        return jnp.concatenate([k_ref[0, pl.ds(start, tk), :],
                                aug_ref[0, 0, pl.ds(start, tk), :]], axis=1)

    def scores(lhs, cols, s_ref, t_ref):
        s = jnp.dot(lhs, qc_ref[:, cols], preferred_element_type=F32)
        s_ref[:, cols] = s
        t_ref[:, cols] = jnp.max(s, axis=0, keepdims=True)

    def softmax_pv(start, cols, s_ref, t_ref):
        h = cols.start // tq
        qcols = slice(cols.start - h * tq, cols.stop - h * tq)
        m_old = m_ref[:, cols]
        m_new = jnp.maximum(m_old, t_ref[:, cols])
        alpha = jnp.exp2(m_old - m_new)
        pb = jnp.exp2(s_ref[:, cols] - m_new).astype(BF16)
        m_ref[:, cols] = m_new
        acc_ref[h, :, qcols] = alpha * acc_ref[h, :, qcols] + pv(pb, start, h)

    sets = (((sa_ref, ta_ref), (sb_ref, tb_ref)), ((sc_ref, tc_ref), (sd_ref, td_ref)))
    col_blocks = [slice(c, c + QCOL_BLOCK) for c in range(0, n_h * tq, QCOL_BLOCK)]

    def pair_body(p, cur, nxt):
        base = pl.multiple_of(p * (2 * tk), 2 * tk)
        for t in range(2):
            start = base + t * tk
            if nxt is not None:
                lhs = key_rows(start + 2 * tk)
            for cols in col_blocks:
                if nxt is not None:
                    scores(lhs, cols, *nxt[t])
                softmax_pv(start, cols, *cur[t])

    n_sub = tq // tk
    d0 = pl.multiple_of(qi * tq, tq)
    kk = lax.broadcasted_iota(jnp.int32, (tk, tk), 0)
    qq = lax.broadcasted_iota(jnp.int32, (tk, tk), 1)
    causal = jnp.where(kk <= qq, 0.0, NEG_BIG)
    blocks = [(h, c) for c in range(n_sub) for h in range(n_h)]
    sq = {}
    for r in range(n_sub):
        lhs = key_rows(d0 + r * tk)
        for h, c in blocks:
            if c >= r:
                cols = slice(h * tq + c * tk, h * tq + (c + 1) * tk)
                s = jnp.dot(lhs, qc_ref[:, cols], preferred_element_type=F32)
                sq[h, c, r] = s + causal if r == c else s
    first_pair = [(t, cols) for t in range(2) for cols in col_blocks]
    first_lhs = [key_rows(t * tk) for t in range(2)]
    state = {}
    n_done = 0
    for r in range(n_sub):
        for h, c in blocks:
            if c < r:
                continue
            s = sq[h, c, r]
            t = jnp.max(s, axis=0, keepdims=True)
            if r == 0:
                m_new = t
            else:
                m_old, acc = state[h, c]
                m_new = jnp.maximum(m_old, t)
                alpha = jnp.exp2(m_old - m_new)
            upd = pv(jnp.exp2(s - m_new).astype(BF16), d0 + r * tk, h)
            acc = upd if r == 0 else alpha * acc + upd
            state[h, c] = (m_new, acc)
            if r == c:
                m_ref[:, h * tq + c * tk:h * tq + (c + 1) * tk] = m_new
                acc_ref[h, :, c * tk:(c + 1) * tk] = acc
            n_done += 1
            if n_done % 2 == 0 and first_pair:
                t0, cols0 = first_pair.pop(0)
                scores(first_lhs[t0], cols0, *sets[0][t0])
    for t0, cols0 in first_pair:
        scores(first_lhs[t0], cols0, *sets[0][t0])

    def pair(p, carry):
        for parity in range(2):
            @pl.when(p % 2 == parity)
            def _(parity=parity):
                pair_body(p, sets[parity], sets[1 - parity])
        return carry

    lax.fori_loop(0, n_pairs - 1, pair, 0)

    for parity in range(2):
        @pl.when((n_pairs > 0) & ((n_pairs - 1) % 2 == parity))
        def _(parity=parity):
            pair_body(n_pairs - 1, sets[parity], None)

    ot = jnp.concatenate(
        [acc_ref[h, :HEAD_DIM, :] * (1.0 / acc_ref[h, HEAD_DIM:HEAD_DIM + 1, :])
         for h in range(n_h)], axis=0)
    o_ref[0] = ot.T.astype(BF16)


def _out_proj_kernel(ys_ref, oa_ref, za_ref, ga_ref, w_ref, x_ref, gp_ref, out_ref, *, ssm_w):
    z = za_ref[...].astype(F32)
    ya = _rms(oa_ref[...].astype(F32) * (z * jax.nn.sigmoid(z)), ga_ref[...]).astype(BF16)
    y = (jnp.dot(ys_ref[...], w_ref[:ssm_w, :], preferred_element_type=F32)
         + jnp.dot(ya, w_ref[ssm_w:, :], preferred_element_type=F32))
    out_ref[...] = x_ref[...] + _rms(y, gp_ref[...])


def _s5_params(a_re, a_im, log_dt, b_re, b_im, c_re, c_im, nj):
    g, n = a_re.shape
    h = b_re.shape[-1]
    n_sg = g // SG_GROUPS
    dt = jnp.exp(log_dt)[:, None]
    mag, ang = jnp.exp(dt * a_re), dt * a_im
    abar_re, abar_im = mag * jnp.cos(ang), mag * jnp.sin(ang)
    den = a_re * a_re + a_im * a_im
    nr, ni = abar_re - 1.0, abar_im
    coef_re = (nr * a_re + ni * a_im) / den
    coef_im = (ni * a_re - nr * a_im) / den
    bbar_re = coef_re[..., None] * b_re - coef_im[..., None] * b_im
    bbar_im = coef_re[..., None] * b_im + coef_im[..., None] * b_re
    eye = jnp.eye(SG_GROUPS, dtype=F32)
    bb = jnp.stack([bbar_re, bbar_im]).reshape(2, n_sg, SG_GROUPS, n, h)
    bsg = jnp.einsum('pignh,gk->ighpkn', bb, eye).reshape(n_sg, SG_GROUPS * h, 2 * SG_GROUPS * n)
    cc = jnp.stack([c_re, -c_im]).reshape(2, n_sg, SG_GROUPS, h, n)
    csg = jnp.einsum('pighn,gk->ipgnkh', cc, eye).reshape(n_sg, 2 * SG_GROUPS * n, SG_GROUPS * h)

    def apow(m):
        m = jnp.asarray(m, F32)[:, None, None]
        pm, pa = jnp.exp(m * (dt * a_re)), m * ang
        return (pm * jnp.cos(pa)).reshape(-1, g * n), (pm * jnp.sin(pa)).reshape(-1, g * n)

    rows = jnp.arange(SUBLANES)
    coef = []
    for k in (1, 2, 4):
        pr, pi = apow([nj * k])
        live = (rows >= k)[:, None]
        coef += [jnp.where(live, pr, 0.0), jnp.where(live, pi, 0.0)]
    pr, pi = apow(nj * (rows + 1))
    coef += [pr, pi]
    pr, pi = apow(jnp.ones((SUBLANES,)))
    coef += [pr, pi]
    return bsg.astype(BF16), csg.astype(BF16), jnp.stack(coef)


def _layer(x, norm_pre_g, w_in, b_forget, a_re, a_im, log_dt, b_re, b_im, c_re, c_im, d_skip,
           w_glu, b_glu, g_ssm, g_attn, w_out, norm_post_g):
    bsz, seq, d_model = x.shape
    n_heads = b_forget.shape[0]
    attn_w = n_heads * HEAD_DIM
    ssm_w = d_skip.shape[0]
    tokens = bsz * seq
    tm, tl, tq = TM_PROJ, TL_SSM, TQ_ATTN
    assert seq % tm == 0 and seq % tl == 0 and seq % tq == 0
    assert ssm_w % LANES == 0 and attn_w % LANES == 0 and n_heads <= 2 * SUBLANES
    nt = seq // tm
    x2 = x.reshape(tokens, d_model)
    row = lambda v: v.reshape(1, -1).astype(F32)

    c0 = 2 * ssm_w
    w_q, w_k, w_v, w_za, w_f = (w_in[:, c0 + j * attn_w:c0 + (j + 1) * attn_w] for j in range(5))
    w_main = jnp.concatenate([w_in[:, :c0], w_k, w_za], axis=1).astype(BF16)
    w_t = jnp.concatenate([w_q.T, w_v.T, w_f.T,
                           jnp.zeros((2 * SUBLANES - n_heads, d_model), F32)], axis=0).astype(BF16)

    uz, k, za, qt, vt, ft = pl.pallas_call(
        functools.partial(_in_proj_kernel, ssm_w=ssm_w, attn_w=attn_w, n_heads=n_heads),
        grid=(tokens // tm,),
        in_specs=[pl.BlockSpec((tm, d_model), lambda i: (i, 0)),
                  pl.BlockSpec((1, d_model), lambda i: (0, 0)),
                  pl.BlockSpec(w_main.shape, lambda i: (0, 0)),
                  pl.BlockSpec(w_t.shape, lambda i: (0, 0))],
        out_specs=[pl.BlockSpec((tm, 2 * ssm_w), lambda i: (i, 0)),
                   pl.BlockSpec((tm, attn_w), lambda i: (i, 0)),
                   pl.BlockSpec((tm, attn_w), lambda i: (i, 0)),
                   pl.BlockSpec((1, attn_w, tm), lambda i: (i // nt, 0, i % nt)),
                   pl.BlockSpec((1, attn_w, tm), lambda i: (i // nt, 0, i % nt)),
                   pl.BlockSpec((1, n_heads, tm), lambda i: (i // nt, 0, i % nt))],
        out_shape=[jax.ShapeDtypeStruct((tokens, 2 * ssm_w), BF16),
                   jax.ShapeDtypeStruct((tokens, attn_w), BF16),
                   jax.ShapeDtypeStruct((tokens, attn_w), BF16),
                   jax.ShapeDtypeStruct((bsz, attn_w, seq), BF16),
                   jax.ShapeDtypeStruct((bsz, attn_w, seq), BF16),
                   jax.ShapeDtypeStruct((bsz, n_heads, seq), F32)],
        compiler_params=_cparams("arbitrary"),
        name="in_proj",
    )(x2, row(norm_pre_g), w_main, w_t)

    n_hp = n_heads // (LANES // HEAD_DIM)
    aug = pl.pallas_call(
        functools.partial(_forget_cum_kernel, seq=seq, n_hp=n_hp, chunk=CUM_CHUNK),
        grid=(bsz,),
        in_specs=[pl.BlockSpec((1, n_heads, seq), lambda b: (b, 0, 0)),
                  pl.BlockSpec((n_heads, 1), lambda b: (0, 0))],
        out_specs=pl.BlockSpec((1, n_hp, seq, LANES), lambda b: (b, 0, 0, 0)),
        out_shape=jax.ShapeDtypeStruct((bsz, n_hp, seq, LANES), BF16),
        compiler_params=_cparams("arbitrary"),
        name="forget_cum",
    )(ft, b_forget.reshape(n_heads, 1).astype(F32))

    assert tl % (2 * SUBLANES * SUBLANES) == 0 and (tl // SUBLANES) & (tl // SUBLANES - 1) == 0
    bsg, csg, coef = _s5_params(a_re, a_im, log_dt, b_re, b_im, c_re, c_im, tl // SUBLANES)
    n_state = a_re.size
    nl = seq // tl
    src_row = (jnp.arange(tl) % SUBLANES) * (tl // SUBLANES) + jnp.arange(tl) // SUBLANES
    perm = (src_row[:, None] == jnp.arange(tl)[None, :])
    perms = jnp.stack([perm, perm.T]).astype(BF16)
    ys = pl.pallas_call(
        functools.partial(_s5_kernel, ssm_w=ssm_w, tl=tl),
        grid=(bsz, nl),
        in_specs=[pl.BlockSpec((tl, 2 * ssm_w), lambda b, t: (b * nl + t, 0)),
                  pl.BlockSpec(perms.shape, lambda b, t: (0, 0, 0)),
                  pl.BlockSpec(bsg.shape, lambda b, t: (0, 0, 0)),
                  pl.BlockSpec(csg.shape, lambda b, t: (0, 0, 0)),
                  pl.BlockSpec(coef.shape, lambda b, t: (0, 0, 0)),
                  pl.BlockSpec((1, ssm_w), lambda b, t: (0, 0)),
                  pl.BlockSpec((ssm_w, ssm_w), lambda b, t: (0, 0)),
                  pl.BlockSpec((1, ssm_w), lambda b, t: (0, 0)),
                  pl.BlockSpec((1, ssm_w), lambda b, t: (0, 0))],
        out_specs=pl.BlockSpec((tl, ssm_w), lambda b, t: (b * nl + t, 0)),
        out_shape=jax.ShapeDtypeStruct((tokens, ssm_w), BF16),
        scratch_shapes=[pltpu.VMEM((tl, 2 * n_state), F32),
                        pltpu.VMEM((tl, 2 * n_state), BF16),
                        pltpu.VMEM((2, n_state), F32)],
        compiler_params=_cparams("arbitrary", "arbitrary"),
        name="s5",
    )(uz, perms, bsg, csg, coef, row(d_skip), w_glu.astype(BF16), row(b_glu), row(g_ssm))

    tk = TK_ATTN
    assert tq % (2 * tk) == 0 and 3 * (LANES // HEAD_DIM) <= LANES
    nq = seq // tq
    wq = (LANES // HEAD_DIM) * tq
    oa = pl.pallas_call(
        functools.partial(_attn_kernel, tq=tq, tk=tk),
        grid=(bsz, n_hp, nq),
        in_specs=[pl.BlockSpec((1, LANES, tq), lambda b, p, i: (b, p, i)),
                  pl.BlockSpec((1, seq, LANES), lambda b, p, i: (b, 0, p)),
                  pl.BlockSpec((1, 1, seq, LANES), lambda b, p, i: (b, p, 0, 0)),
                  pl.BlockSpec((1, LANES, seq), lambda b, p, i: (b, p, 0))],
        out_specs=pl.BlockSpec((1, tq, LANES), lambda b, p, i: (b, i, p)),
        out_shape=jax.ShapeDtypeStruct((bsz, seq, attn_w), BF16),
        scratch_shapes=[pltpu.VMEM((2 * LANES, wq), BF16),
                        pltpu.VMEM((tk, wq + LANES), F32),
                        pltpu.VMEM((tk, wq + LANES), F32),
                        pltpu.VMEM((tk, wq + LANES), F32),
                        pltpu.VMEM((tk, wq + LANES), F32),
                        pltpu.VMEM((1, wq), F32),
                        pltpu.VMEM((1, wq), F32),
                        pltpu.VMEM((1, wq), F32),
                        pltpu.VMEM((1, wq), F32),
                        pltpu.VMEM((1, wq), F32),
                        pltpu.VMEM((LANES // HEAD_DIM, HEAD_DIM + PV_ONES_ROWS, tq), F32)],
        compiler_params=_cparams("arbitrary", "arbitrary", "arbitrary"),
        name="fox_attn",
    )(qt, k.reshape(bsz, seq, attn_w), aug, vt)

    out = pl.pallas_call(
        functools.partial(_out_proj_kernel, ssm_w=ssm_w),
        grid=(tokens // tm,),
        in_specs=[pl.BlockSpec((tm, ssm_w), lambda i: (i, 0)),
                  pl.BlockSpec((tm, attn_w), lambda i: (i, 0)),
                  pl.BlockSpec((tm, attn_w), lambda i: (i, 0)),
                  pl.BlockSpec((1, attn_w), lambda i: (0, 0)),
                  pl.BlockSpec((ssm_w + attn_w, d_model), lambda i: (0, 0)),
                  pl.BlockSpec((tm, d_model), lambda i: (i, 0)),
                  pl.BlockSpec((1, d_model), lambda i: (0, 0))],
        out_specs=pl.BlockSpec((tm, d_model), lambda i: (i, 0)),
        out_shape=jax.ShapeDtypeStruct((tokens, d_model), F32),
        compiler_params=_cparams("arbitrary"),
        name="out_proj",
    )(ys, oa.reshape(tokens, attn_w), za, row(g_attn), w_out.astype(BF16), x2, row(norm_post_g))
    return out.reshape(bsz, seq, d_model)


def kernel(x, norm_pre_g, w_in, b_forget, ssm_a_re, ssm_a_im, ssm_log_dt, ssm_b_re, ssm_b_im,
           ssm_c_re, ssm_c_im, ssm_d, w_glu, b_glu, g_ssm, g_attn, w_out, norm_post_g):
    h = x
    for layer in range(norm_pre_g.shape[0]):
        h = _layer(h, norm_pre_g[layer], w_in[layer], b_forget[layer], ssm_a_re[layer],
                   ssm_a_im[layer], ssm_log_dt[layer], ssm_b_re[layer], ssm_b_im[layer],
                   ssm_c_re[layer], ssm_c_im[layer], ssm_d[layer], w_glu[layer], b_glu[layer],
                   g_ssm[layer], g_attn[layer], w_out[layer], norm_post_g[layer])
    return h
```

```python
import functools
import math

import jax
import jax.numpy as jnp
from jax import lax
from jax.experimental import pallas as pl
from jax.experimental.pallas import tpu as pltpu

F32 = jnp.float32
BF16 = jnp.bfloat16

SSM_GROUP_CH = 16
SSM_STATE = 64
HEAD_DIM = 64
RMS_EPS = 1e-6

LANES = 128
SUBLANES = 8
VMEM_LIMIT_BYTES = 56 * 1024 * 1024

TM_PROJ = 512
TM_OUT = 1024
TL_SSM = 256
TQ_ATTN = 1024
TK_ATTN = 512
TD_ATTN = 256
QCOL_BLOCK = 512
CUM_CHUNK = 128
PV_ONES_ROWS = 2 * SUBLANES
LOG2E = math.log2(math.e)
NT_DIMS = (((1,), (1,)), ((), ()))
SG_GROUPS = LANES // SSM_GROUP_CH
SG_STATE = SG_GROUPS * SSM_STATE
SCAN_W = 256
NEG_BIG = -1e30


def _cparams(*sem):
    return pltpu.CompilerParams(dimension_semantics=sem, vmem_limit_bytes=VMEM_LIMIT_BYTES)


def _rms(v, g):
    return v * lax.rsqrt(jnp.mean(v * v, axis=-1, keepdims=True) + RMS_EPS) * g


def _in_proj_kernel(x_ref, g_ref, w_ref, wt_ref, uz_ref, k_ref, za_ref, qt_ref, vt_ref, ft_ref,
                    *, ssm_w, attn_w, n_heads):
    xb = _rms(x_ref[...], g_ref[...]).astype(BF16)
    c0 = 2 * ssm_w
    uz_ref[...] = jnp.dot(xb, w_ref[:, :c0], preferred_element_type=F32).astype(BF16)
    k_ref[...] = jnp.dot(xb, w_ref[:, c0:c0 + attn_w], preferred_element_type=F32).astype(BF16)
    za_ref[...] = jnp.dot(xb, w_ref[:, c0 + attn_w:], preferred_element_type=F32).astype(BF16)
    qt = lax.dot_general(wt_ref[:attn_w], xb, NT_DIMS, preferred_element_type=F32)
    qt_ref[0] = (qt * (HEAD_DIM ** -0.5 * LOG2E)).astype(BF16)
    vt_ref[0] = lax.dot_general(wt_ref[attn_w:2 * attn_w], xb, NT_DIMS,
                                preferred_element_type=F32).astype(BF16)
    ft = lax.dot_general(wt_ref[2 * attn_w:], xb, NT_DIMS, preferred_element_type=F32)
    ft_ref[0] = ft[:n_heads]


def _forget_cum_kernel(ft_ref, bf_ref, aug_ref, *, seq, n_hp, chunk):
    f = ft_ref[0] + bf_ref[...]
    lf = jnp.minimum(f, 0.0) - jnp.log1p(jnp.exp(-jnp.abs(f)))
    lane = lax.broadcasted_iota(jnp.int32, lf.shape, 1)
    k = 1
    while k < seq:
        lf = lf + jnp.where(lane >= k, pltpu.roll(lf, k, axis=1), 0.0)
        k *= 2
    nb = lf * (-LOG2E)
    hi = nb.astype(BF16).astype(F32)
    mid = (nb - hi).astype(BF16).astype(F32)
    lo = (nb - hi - mid).astype(BF16).astype(F32)
    parts = (hi, mid, lo)
    rid = lax.broadcasted_iota(jnp.int32, (SUBLANES, seq), 0)
    eye = (lax.broadcasted_iota(jnp.int32, (chunk, chunk), 0)
           == lax.broadcasted_iota(jnp.int32, (chunk, chunk), 1)).astype(BF16)
    pad = jnp.zeros((LANES - SUBLANES, seq), F32)
    for p in range(n_hp):
        rows = jnp.zeros((SUBLANES, seq), F32)
        for j in range(LANES // HEAD_DIM):
            for t, part in enumerate(parts):
                src = part[2 * p + j:2 * p + j + 1, :]
                rows = jnp.where(rid == 3 * j + t, jnp.broadcast_to(src, rows.shape), rows)
        bmat = jnp.concatenate([rows, pad], axis=0).astype(BF16)
        for c in range(seq // chunk):
            blk = lax.dot_general(eye, bmat[:, c * chunk:(c + 1) * chunk], NT_DIMS,
                                  preferred_element_type=F32)
            aug_ref[0, p, c * chunk:(c + 1) * chunk, :] = blk.astype(BF16)


def _cmul(ar, ai, xr, xi):
    return ar * xr - ai * xi, ar * xi + ai * xr


def _s5_kernel(uz_ref, perm_ref, bsg_ref, csg_ref, coef_ref, d_ref, wglu_ref, bglu_ref, gs_ref,
               ys_ref, bu_ref, hb_ref, carry_ref, *, ssm_w, tl):
    n_sg = ssm_w // LANES
    sg_cols = 2 * SG_STATE
    nj = tl // SUBLANES

    @pl.when(pl.program_id(1) == 0)
    def _():
        carry_ref[...] = jnp.zeros_like(carry_ref)

    uzp = jnp.dot(perm_ref[0], uz_ref[...], preferred_element_type=F32).astype(BF16)

    rid = lax.broadcasted_iota(jnp.int32, (SUBLANES, SG_STATE), 0)
    for i in range(n_sg):
        re = slice(i * sg_cols, i * sg_cols + SG_STATE)
        im = slice(i * sg_cols + SG_STATE, (i + 1) * sg_cols)
        cs = slice(i * SG_STATE, (i + 1) * SG_STATE)
        bu_ref[:, i * sg_cols:(i + 1) * sg_cols] = jnp.dot(
            uzp[:, i * LANES:(i + 1) * LANES], bsg_ref[i], preferred_element_type=F32)
        ar, ai = coef_ref[8, :, cs], coef_ref[9, :, cs]

        hr = jnp.zeros((SUBLANES, SG_STATE), F32)
        hi = jnp.zeros((SUBLANES, SG_STATE), F32)
        for j in range(nj):
            rows = slice(j * SUBLANES, (j + 1) * SUBLANES)
            tr, ti = _cmul(ar, ai, hr, hi)
            hr, hi = tr + bu_ref[rows, re], ti + bu_ref[rows, im]
            bu_ref[rows, re] = hr
            bu_ref[rows, im] = hi

        for t in range(3):
            k = 1 << t
            tr, ti = _cmul(coef_ref[2 * t, :, cs], coef_ref[2 * t + 1, :, cs],
                           pltpu.roll(hr, k, axis=0), pltpu.roll(hi, k, axis=0))
            hr, hi = hr + tr, hi + ti
        cr = jnp.broadcast_to(carry_ref[0:1, cs], hr.shape)
        cim = jnp.broadcast_to(carry_ref[1:2, cs], hi.shape)
        tr, ti = _cmul(coef_ref[6, :, cs], coef_ref[7, :, cs], cr, cim)
        hr, hi = hr + tr, hi + ti
        carry_ref[0:1, cs] = hr[SUBLANES - 1:SUBLANES, :]
        carry_ref[1:2, cs] = hi[SUBLANES - 1:SUBLANES, :]
        gr = jnp.where(rid == 0, cr, pltpu.roll(hr, 1, axis=0))
        gi = jnp.where(rid == 0, cim, pltpu.roll(hi, 1, axis=0))

        for j in range(0, nj, 2):
            outs = []
            for jj in (j, j + 1):
                rows = slice(jj * SUBLANES, (jj + 1) * SUBLANES)
                gr, gi = _cmul(ar, ai, gr, gi)
                outs.append((bu_ref[rows, re] + gr, bu_ref[rows, im] + gi))
            rows2 = slice(j * SUBLANES, (j + 2) * SUBLANES)
            hb_ref[rows2, re] = jnp.concatenate([outs[0][0], outs[1][0]], axis=0).astype(BF16)
            hb_ref[rows2, im] = jnp.concatenate([outs[0][1], outs[1][1]], axis=0).astype(BF16)

    y = jnp.concatenate(
        [jnp.dot(hb_ref[:, i * sg_cols:(i + 1) * sg_cols], csg_ref[i],
                 preferred_element_type=F32) for i in range(n_sg)], axis=-1)
    u = uzp[:, :ssm_w].astype(F32)
    z = uzp[:, ssm_w:].astype(F32)
    y = jax.nn.gelu(y + d_ref[...] * u)
    gate = jnp.dot(y.astype(BF16), wglu_ref[...], preferred_element_type=F32) + bglu_ref[...]
    y = y * jax.nn.sigmoid(gate)
    y = y * (z * jax.nn.sigmoid(z))
    ysp = _rms(y, gs_ref[...]).astype(BF16)
    ys_ref[...] = jnp.dot(perm_ref[1], ysp, preferred_element_type=F32).astype(BF16)


def _attn_kernel(qt_ref, k_ref, aug_ref, vt_ref, o_ref, qc_ref, sa_ref, sb_ref, sc_ref, sd_ref,
                 ta_ref, tb_ref, tc_ref, td_ref, m_ref, acc_ref, *, tq, tk, td):
    qi = pl.program_id(2)
    n_h = LANES // HEAD_DIM

    zero = jnp.zeros((HEAD_DIM, tq), BF16)
    for h in range(n_h):
        for hh in range(n_h):
            qc_ref[h * HEAD_DIM:(h + 1) * HEAD_DIM, hh * tq:(hh + 1) * tq] = (
                qt_ref[0, h * HEAD_DIM:(h + 1) * HEAD_DIM, :] if h == hh else zero)
    r = lax.broadcasted_iota(jnp.int32, (LANES, n_h * tq), 0)
    c = lax.broadcasted_iota(jnp.int32, (LANES, n_h * tq), 1)
    ones = (r >= 0) & (r < 3) & (c < tq)
    for h in range(1, n_h):
        ones = ones | ((r >= 3 * h) & (r < 3 * h + 3) & (c >= h * tq) & (c < (h + 1) * tq))
    qc_ref[LANES:, :] = jnp.where(ones, 1.0, 0.0).astype(BF16)

    def pv(pb, start, h):
        size = pb.shape[0]
        lhs = jnp.concatenate([vt_ref[0, h * HEAD_DIM:(h + 1) * HEAD_DIM, pl.ds(start, size)],
                               jnp.ones((PV_ONES_ROWS, size), BF16)], axis=0)
        return jnp.dot(lhs, pb, preferred_element_type=F32)

    n_pairs = qi * (tq // (2 * tk))

    def key_rows(start, size=tk):
        return jnp.concatenate([k_ref[0, pl.ds(start, size), :],
                                aug_ref[0, 0, pl.ds(start, size), :]], axis=1)

    def scores(lhs, cols, s_ref, t_ref):
        s = jnp.dot(lhs, qc_ref[:, cols], preferred_element_type=F32)
        s_ref[:, cols] = s
        t_ref[:, cols] = jnp.max(s, axis=0, keepdims=True)

    def softmax_pv(start, cols, s_ref, t_ref):
        h = cols.start // tq
        qcols = slice(cols.start - h * tq, cols.stop - h * tq)
        m_old = m_ref[:, cols]
        m_new = jnp.maximum(m_old, t_ref[:, cols])
        alpha = jnp.exp2(m_old - m_new)
        pb = jnp.exp2(s_ref[:, cols] - m_new).astype(BF16)
        m_ref[:, cols] = m_new
        acc_ref[h, :, qcols] = alpha * acc_ref[h, :, qcols] + pv(pb, start, h)

    sets = (((sa_ref, ta_ref), (sb_ref, tb_ref)), ((sc_ref, tc_ref), (sd_ref, td_ref)))
    col_blocks = [slice(c, c + QCOL_BLOCK) for c in range(0, n_h * tq, QCOL_BLOCK)]

    def pair_body(p, cur, nxt):
        base = pl.multiple_of(p * (2 * tk), 2 * tk)
        for t in range(2):
            start = base + t * tk
            if nxt is not None:
                lhs = key_rows(start + 2 * tk)
            for cols in col_blocks:
                if nxt is not None:
                    scores(lhs, cols, *nxt[t])
                softmax_pv(start, cols, *cur[t])

    n_sub = tq // td
    d0 = pl.multiple_of(qi * tq, tq)
    kk = lax.broadcasted_iota(jnp.int32, (td, td), 0)
    qq = lax.broadcasted_iota(jnp.int32, (td, td), 1)
    causal = jnp.where(kk <= qq, 0.0, NEG_BIG)
    blocks = [(h, c) for c in range(n_sub) for h in range(n_h)]
    sq = {}
    for r in range(n_sub):
        lhs = key_rows(d0 + r * td, td)
        for h, c in blocks:
            if c >= r:
                cols = slice(h * tq + c * td, h * tq + (c + 1) * td)
                s = jnp.dot(lhs, qc_ref[:, cols], preferred_element_type=F32)
                sq[h, c, r] = s + causal if r == c else s
    first_pair = [(t, cols) for t in range(2) for cols in col_blocks]
    first_lhs = [key_rows(t * tk) for t in range(2)]
    state = {}
    n_done = 0
    for r in range(n_sub):
        for h, c in blocks:
            if c < r:
                continue
            s = sq[h, c, r]
            t = jnp.max(s, axis=0, keepdims=True)
            if r == 0:
                m_new = t
            else:
                m_old, acc = state[h, c]
                m_new = jnp.maximum(m_old, t)
                alpha = jnp.exp2(m_old - m_new)
            upd = pv(jnp.exp2(s - m_new).astype(BF16), d0 + r * td, h)
            acc = upd if r == 0 else alpha * acc + upd
            state[h, c] = (m_new, acc)
            if r == c:
                m_ref[:, h * tq + c * td:h * tq + (c + 1) * td] = m_new
                acc_ref[h, :, c * td:(c + 1) * td] = acc
            n_done += 1
            if n_done % 2 == 0 and first_pair:
                t0, cols0 = first_pair.pop(0)
                scores(first_lhs[t0], cols0, *sets[0][t0])
    for t0, cols0 in first_pair:
        scores(first_lhs[t0], cols0, *sets[0][t0])

    def pair(p, carry):
        for parity in range(2):
            @pl.when(p % 2 == parity)
            def _(parity=parity):
                pair_body(p, sets[parity], sets[1 - parity])
        return carry

    lax.fori_loop(0, n_pairs - 1, pair, 0)

    for parity in range(2):
        @pl.when((n_pairs > 0) & ((n_pairs - 1) % 2 == parity))
        def _(parity=parity):
            pair_body(n_pairs - 1, sets[parity], None)

    ot = jnp.concatenate(
        [acc_ref[h, :HEAD_DIM, :] * (1.0 / acc_ref[h, HEAD_DIM:HEAD_DIM + 1, :])
         for h in range(n_h)], axis=0)
    o_ref[0] = ot.T.astype(BF16)


def _out_proj_kernel(ys_ref, oa_ref, za_ref, ga_ref, w_ref, x_ref, gp_ref, out_ref, *, ssm_w):
    z = za_ref[...].astype(F32)
    ya = _rms(oa_ref[...].astype(F32) * (z * jax.nn.sigmoid(z)), ga_ref[...]).astype(BF16)
    y = (jnp.dot(ys_ref[...], w_ref[:ssm_w, :], preferred_element_type=F32)
         + jnp.dot(ya, w_ref[ssm_w:, :], preferred_element_type=F32))
    out_ref[...] = x_ref[...] + _rms(y, gp_ref[...])


def _s5_params(a_re, a_im, log_dt, b_re, b_im, c_re, c_im, nj):
    g, n = a_re.shape
    h = b_re.shape[-1]
    n_sg = g // SG_GROUPS
    dt = jnp.exp(log_dt)[:, None]
    mag, ang = jnp.exp(dt * a_re), dt * a_im
    abar_re, abar_im = mag * jnp.cos(ang), mag * jnp.sin(ang)
    den = a_re * a_re + a_im * a_im
    nr, ni = abar_re - 1.0, abar_im
    coef_re = (nr * a_re + ni * a_im) / den
    coef_im = (ni * a_re - nr * a_im) / den
    bbar_re = coef_re[..., None] * b_re - coef_im[..., None] * b_im
    bbar_im = coef_re[..., None] * b_im + coef_im[..., None] * b_re
    eye = jnp.eye(SG_GROUPS, dtype=F32)
    bb = jnp.stack([bbar_re, bbar_im]).reshape(2, n_sg, SG_GROUPS, n, h)
    bsg = jnp.einsum('pignh,gk->ighpkn', bb, eye).reshape(n_sg, SG_GROUPS * h, 2 * SG_GROUPS * n)
    cc = jnp.stack([c_re, -c_im]).reshape(2, n_sg, SG_GROUPS, h, n)
    csg = jnp.einsum('pighn,gk->ipgnkh', cc, eye).reshape(n_sg, 2 * SG_GROUPS * n, SG_GROUPS * h)

    def apow(m):
        m = jnp.asarray(m, F32)[:, None, None]
        pm, pa = jnp.exp(m * (dt * a_re)), m * ang
        return (pm * jnp.cos(pa)).reshape(-1, g * n), (pm * jnp.sin(pa)).reshape(-1, g * n)

    rows = jnp.arange(SUBLANES)
    coef = []
    for k in (1, 2, 4):
        pr, pi = apow([nj * k])
        live = (rows >= k)[:, None]
        coef += [jnp.where(live, pr, 0.0), jnp.where(live, pi, 0.0)]
    pr, pi = apow(nj * (rows + 1))
    coef += [pr, pi]
    pr, pi = apow(jnp.ones((SUBLANES,)))
    coef += [pr, pi]
    return bsg.astype(BF16), csg.astype(BF16), jnp.stack(coef)


def _layer(x, norm_pre_g, w_in, b_forget, a_re, a_im, log_dt, b_re, b_im, c_re, c_im, d_skip,
           w_glu, b_glu, g_ssm, g_attn, w_out, norm_post_g):
    bsz, seq, d_model = x.shape
    n_heads = b_forget.shape[0]
    attn_w = n_heads * HEAD_DIM
    ssm_w = d_skip.shape[0]
    tokens = bsz * seq
    tm, to, tl, tq = TM_PROJ, TM_OUT, TL_SSM, TQ_ATTN
    assert seq % tm == 0 and tokens % to == 0 and seq % tl == 0 and seq % tq == 0
    assert ssm_w % LANES == 0 and attn_w % LANES == 0 and n_heads <= 2 * SUBLANES
    nt = seq // tm
    x2 = x.reshape(tokens, d_model)
    row = lambda v: v.reshape(1, -1).astype(F32)

    c0 = 2 * ssm_w
    w_q, w_k, w_v, w_za, w_f = (w_in[:, c0 + j * attn_w:c0 + (j + 1) * attn_w] for j in range(5))
    w_main = jnp.concatenate([w_in[:, :c0], w_k, w_za], axis=1).astype(BF16)
    w_t = jnp.concatenate([w_q.T, w_v.T, w_f.T,
                           jnp.zeros((2 * SUBLANES - n_heads, d_model), F32)], axis=0).astype(BF16)

    uz, k, za, qt, vt, ft = pl.pallas_call(
        functools.partial(_in_proj_kernel, ssm_w=ssm_w, attn_w=attn_w, n_heads=n_heads),
        grid=(tokens // tm,),
        in_specs=[pl.BlockSpec((tm, d_model), lambda i: (i, 0)),
                  pl.BlockSpec((1, d_model), lambda i: (0, 0)),
                  pl.BlockSpec(w_main.shape, lambda i: (0, 0)),
                  pl.BlockSpec(w_t.shape, lambda i: (0, 0))],
        out_specs=[pl.BlockSpec((tm, 2 * ssm_w), lambda i: (i, 0)),
                   pl.BlockSpec((tm, attn_w), lambda i: (i, 0)),
                   pl.BlockSpec((tm, attn_w), lambda i: (i, 0)),
                   pl.BlockSpec((1, attn_w, tm), lambda i: (i // nt, 0, i % nt)),
                   pl.BlockSpec((1, attn_w, tm), lambda i: (i // nt, 0, i % nt)),
                   pl.BlockSpec((1, n_heads, tm), lambda i: (i // nt, 0, i % nt))],
        out_shape=[jax.ShapeDtypeStruct((tokens, 2 * ssm_w), BF16),
                   jax.ShapeDtypeStruct((tokens, attn_w), BF16),
                   jax.ShapeDtypeStruct((tokens, attn_w), BF16),
                   jax.ShapeDtypeStruct((bsz, attn_w, seq), BF16),
                   jax.ShapeDtypeStruct((bsz, attn_w, seq), BF16),
                   jax.ShapeDtypeStruct((bsz, n_heads, seq), F32)],
        compiler_params=_cparams("arbitrary"),
        name="in_proj",
    )(x2, row(norm_pre_g), w_main, w_t)

    n_hp = n_heads // (LANES // HEAD_DIM)
    aug = pl.pallas_call(
        functools.partial(_forget_cum_kernel, seq=seq, n_hp=n_hp, chunk=CUM_CHUNK),
        grid=(bsz,),
        in_specs=[pl.BlockSpec((1, n_heads, seq), lambda b: (b, 0, 0)),
                  pl.BlockSpec((n_heads, 1), lambda b: (0, 0))],
        out_specs=pl.BlockSpec((1, n_hp, seq, LANES), lambda b: (b, 0, 0, 0)),
        out_shape=jax.ShapeDtypeStruct((bsz, n_hp, seq, LANES), BF16),
        compiler_params=_cparams("arbitrary"),
        name="forget_cum",
    )(ft, b_forget.reshape(n_heads, 1).astype(F32))

    assert tl % (2 * SUBLANES * SUBLANES) == 0 and (tl // SUBLANES) & (tl // SUBLANES - 1) == 0
    bsg, csg, coef = _s5_params(a_re, a_im, log_dt, b_re, b_im, c_re, c_im, tl // SUBLANES)
    n_state = a_re.size
    nl = seq // tl
    src_row = (jnp.arange(tl) % SUBLANES) * (tl // SUBLANES) + jnp.arange(tl) // SUBLANES
    perm = (src_row[:, None] == jnp.arange(tl)[None, :])
    perms = jnp.stack([perm, perm.T]).astype(BF16)
    ys = pl.pallas_call(
        functools.partial(_s5_kernel, ssm_w=ssm_w, tl=tl),
        grid=(bsz, nl),
        in_specs=[pl.BlockSpec((tl, 2 * ssm_w), lambda b, t: (b * nl + t, 0)),
                  pl.BlockSpec(perms.shape, lambda b, t: (0, 0, 0)),
                  pl.BlockSpec(bsg.shape, lambda b, t: (0, 0, 0)),
                  pl.BlockSpec(csg.shape, lambda b, t: (0, 0, 0)),
                  pl.BlockSpec(coef.shape, lambda b, t: (0, 0, 0)),
                  pl.BlockSpec((1, ssm_w), lambda b, t: (0, 0)),
                  pl.BlockSpec((ssm_w, ssm_w), lambda b, t: (0, 0)),
                  pl.BlockSpec((1, ssm_w), lambda b, t: (0, 0)),
                  pl.BlockSpec((1, ssm_w), lambda b, t: (0, 0))],
        out_specs=pl.BlockSpec((tl, ssm_w), lambda b, t: (b * nl + t, 0)),
        out_shape=jax.ShapeDtypeStruct((tokens, ssm_w), BF16),
        scratch_shapes=[pltpu.VMEM((tl, 2 * n_state), F32),
                        pltpu.VMEM((tl, 2 * n_state), BF16),
                        pltpu.VMEM((2, n_state), F32)],
        compiler_params=_cparams("arbitrary", "arbitrary"),
        name="s5",
    )(uz, perms, bsg, csg, coef, row(d_skip), w_glu.astype(BF16), row(b_glu), row(g_ssm))

    tk, td = TK_ATTN, TD_ATTN
    assert tq % (2 * tk) == 0 and tq % td == 0 and 3 * (LANES // HEAD_DIM) <= LANES
    nq = seq // tq
    wq = (LANES // HEAD_DIM) * tq
    oa = pl.pallas_call(
        functools.partial(_attn_kernel, tq=tq, tk=tk, td=td),
        grid=(bsz, n_hp, nq),
        in_specs=[pl.BlockSpec((1, LANES, tq), lambda b, p, i: (b, p, i)),
                  pl.BlockSpec((1, seq, LANES), lambda b, p, i: (b, 0, p)),
                  pl.BlockSpec((1, 1, seq, LANES), lambda b, p, i: (b, p, 0, 0)),
                  pl.BlockSpec((1, LANES, seq), lambda b, p, i: (b, p, 0))],
        out_specs=pl.BlockSpec((1, tq, LANES), lambda b, p, i: (b, i, p)),
        out_shape=jax.ShapeDtypeStruct((bsz, seq, attn_w), BF16),
        scratch_shapes=[pltpu.VMEM((2 * LANES, wq), BF16),
                        pltpu.VMEM((tk, wq + LANES), F32),
                        pltpu.VMEM((tk, wq + LANES), F32),
                        pltpu.VMEM((tk, wq + LANES), F32),
                        pltpu.VMEM((tk, wq + LANES), F32),
                        pltpu.VMEM((1, wq), F32),
                        pltpu.VMEM((1, wq), F32),
                        pltpu.VMEM((1, wq), F32),
                        pltpu.VMEM((1, wq), F32),
                        pltpu.VMEM((1, wq), F32),
                        pltpu.VMEM((LANES // HEAD_DIM, HEAD_DIM + PV_ONES_ROWS, tq), F32)],
        compiler_params=_cparams("arbitrary", "arbitrary", "arbitrary"),
        name="fox_attn",
    )(qt, k.reshape(bsz, seq, attn_w), aug, vt)

    out = pl.pallas_call(
        functools.partial(_out_proj_kernel, ssm_w=ssm_w),
        grid=(tokens // to,),
        in_specs=[pl.BlockSpec((to, ssm_w), lambda i: (i, 0)),
                  pl.BlockSpec((to, attn_w), lambda i: (i, 0)),
                  pl.BlockSpec((to, attn_w), lambda i: (i, 0)),
                  pl.BlockSpec((1, attn_w), lambda i: (0, 0)),
                  pl.BlockSpec((ssm_w + attn_w, d_model), lambda i: (0, 0)),
                  pl.BlockSpec((to, d_model), lambda i: (i, 0)),
                  pl.BlockSpec((1, d_model), lambda i: (0, 0))],
        out_specs=pl.BlockSpec((to, d_model), lambda i: (i, 0)),
        out_shape=jax.ShapeDtypeStruct((tokens, d_model), F32),
        compiler_params=_cparams("arbitrary"),
        name="out_proj",
    )(ys, oa.reshape(tokens, attn_w), za, row(g_attn), w_out.astype(BF16), x2, row(norm_post_g))
    return out.reshape(bsz, seq, d_model)


def kernel(x, norm_pre_g, w_in, b_forget, ssm_a_re, ssm_a_im, ssm_log_dt, ssm_b_re, ssm_b_im,
           ssm_c_re, ssm_c_im, ssm_d, w_glu, b_glu, g_ssm, g_attn, w_out, norm_post_g):
    h = x
    for layer in range(norm_pre_g.shape[0]):
        h = _layer(h, norm_pre_g[layer], w_in[layer], b_forget[layer], ssm_a_re[layer],
                   ssm_a_im[layer], ssm_log_dt[layer], ssm_b_re[layer], ssm_b_im[layer],
                   ssm_c_re[layer], ssm_c_im[layer], ssm_d[layer], w_glu[layer], b_glu[layer],
                   g_ssm[layer], g_attn[layer], w_out[layer], norm_post_g[layer])
    return h
```

```python
import functools
import math

import jax
import jax.numpy as jnp
from jax import lax
from jax.experimental import pallas as pl
from jax.experimental.pallas import tpu as pltpu

F32 = jnp.float32
BF16 = jnp.bfloat16

SSM_GROUP_CH = 16
SSM_STATE = 64
HEAD_DIM = 64
RMS_EPS = 1e-6

LANES = 128
SUBLANES = 8
VMEM_LIMIT_BYTES = 56 * 1024 * 1024

TM_PROJ = 512
TM_OUT = 1024
TL_SSM = 256
TQ_ATTN = 1024
TK_ATTN = 512
TD_ATTN = 256
QCOL_BLOCK = 512
CUM_CHUNK = 128
PV_ONES_ROWS = 2 * SUBLANES
LOG2E = math.log2(math.e)
NT_DIMS = (((1,), (1,)), ((), ()))
SG_GROUPS = LANES // SSM_GROUP_CH
SG_STATE = SG_GROUPS * SSM_STATE
SCAN_W = 256
NEG_BIG = -1e30


def _cparams(*sem):
    return pltpu.CompilerParams(dimension_semantics=sem, vmem_limit_bytes=VMEM_LIMIT_BYTES)


def _rms(v, g):
    return v * lax.rsqrt(jnp.mean(v * v, axis=-1, keepdims=True) + RMS_EPS) * g


def _forget_cum_kernel(ft_ref, bf_ref, aug_ref, *, seq, n_hp, chunk):
    f = ft_ref[0] + bf_ref[...]
    lf = jnp.minimum(f, 0.0) - jnp.log1p(jnp.exp(-jnp.abs(f)))
    lane = lax.broadcasted_iota(jnp.int32, lf.shape, 1)
    k = 1
    while k < seq:
        lf = lf + jnp.where(lane >= k, pltpu.roll(lf, k, axis=1), 0.0)
        k *= 2
    nb = lf * (-LOG2E)
    hi = nb.astype(BF16).astype(F32)
    mid = (nb - hi).astype(BF16).astype(F32)
    lo = (nb - hi - mid).astype(BF16).astype(F32)
    parts = (hi, mid, lo)
    rid = lax.broadcasted_iota(jnp.int32, (SUBLANES, seq), 0)
    eye = (lax.broadcasted_iota(jnp.int32, (chunk, chunk), 0)
           == lax.broadcasted_iota(jnp.int32, (chunk, chunk), 1)).astype(BF16)
    pad = jnp.zeros((LANES - SUBLANES, seq), F32)
    for p in range(n_hp):
        rows = jnp.zeros((SUBLANES, seq), F32)
        for j in range(LANES // HEAD_DIM):
            for t, part in enumerate(parts):
                src = part[2 * p + j:2 * p + j + 1, :]
                rows = jnp.where(rid == 3 * j + t, jnp.broadcast_to(src, rows.shape), rows)
        bmat = jnp.concatenate([rows, pad], axis=0).astype(BF16)
        for c in range(seq // chunk):
            blk = lax.dot_general(eye, bmat[:, c * chunk:(c + 1) * chunk], NT_DIMS,
                                  preferred_element_type=F32)
            aug_ref[0, p, c * chunk:(c + 1) * chunk, :] = blk.astype(BF16)


def _cmul(ar, ai, xr, xi):
    return ar * xr - ai * xi, ar * xi + ai * xr


def _in_proj_s5_kernel(x_ref, g_ref, w_ref, wt_ref, perm_ref, bsg_ref, csg_ref, coef_ref, d_ref,
                       wglu_ref, bglu_ref, gs_ref, ys_ref, k_ref, za_ref, qt_ref, vt_ref, ft_ref,
                       bu_ref, hb_ref, carry_ref, *, ssm_w, attn_w, n_heads, tl, tiles_per_seq):
    n_sg = ssm_w // LANES
    sg_cols = 2 * SG_STATE
    nj = tl // SUBLANES
    c0 = 2 * ssm_w

    @pl.when(pl.program_id(0) % tiles_per_seq == 0)
    def _():
        carry_ref[...] = jnp.zeros_like(carry_ref)

    xb = _rms(x_ref[...], g_ref[...]).astype(BF16)
    uz = jnp.dot(xb, w_ref[:, :c0], preferred_element_type=F32).astype(BF16)
    uzp = jnp.dot(perm_ref[0], uz, preferred_element_type=F32).astype(BF16)

    def proj_k():
        k_ref[...] = jnp.dot(xb, w_ref[:, c0:c0 + attn_w],
                             preferred_element_type=F32).astype(BF16)

    def proj_za():
        za_ref[...] = jnp.dot(xb, w_ref[:, c0 + attn_w:], preferred_element_type=F32).astype(BF16)

    def proj_qt():
        qt = lax.dot_general(wt_ref[:attn_w], xb, NT_DIMS, preferred_element_type=F32)
        qt_ref[0] = (qt * (HEAD_DIM ** -0.5 * LOG2E)).astype(BF16)

    def proj_vt():
        vt_ref[0] = lax.dot_general(wt_ref[attn_w:2 * attn_w], xb, NT_DIMS,
                                    preferred_element_type=F32).astype(BF16)
        ft = lax.dot_general(wt_ref[2 * attn_w:], xb, NT_DIMS, preferred_element_type=F32)
        ft_ref[0] = ft[:n_heads]

    fillers = [proj_k, proj_za, proj_qt, proj_vt]

    for i in range(n_sg):
        bu_ref[:, i * sg_cols:(i + 1) * sg_cols] = jnp.dot(
            uzp[:, i * LANES:(i + 1) * LANES], bsg_ref[i], preferred_element_type=F32)

    rid = lax.broadcasted_iota(jnp.int32, (SUBLANES, SG_STATE), 0)
    y_parts = []
    for i in range(n_sg):
        re = slice(i * sg_cols, i * sg_cols + SG_STATE)
        im = slice(i * sg_cols + SG_STATE, (i + 1) * sg_cols)
        cs = slice(i * SG_STATE, (i + 1) * SG_STATE)
        if fillers:
            fillers.pop(0)()
        ar, ai = coef_ref[8, :, cs], coef_ref[9, :, cs]

        hr = jnp.zeros((SUBLANES, SG_STATE), F32)
        hi = jnp.zeros((SUBLANES, SG_STATE), F32)
        for j in range(nj):
            rows = slice(j * SUBLANES, (j + 1) * SUBLANES)
            tr, ti = _cmul(ar, ai, hr, hi)
            hr, hi = tr + bu_ref[rows, re], ti + bu_ref[rows, im]
            bu_ref[rows, re] = hr
            bu_ref[rows, im] = hi

        for t in range(3):
            k = 1 << t
            tr, ti = _cmul(coef_ref[2 * t, :, cs], coef_ref[2 * t + 1, :, cs],
                           pltpu.roll(hr, k, axis=0), pltpu.roll(hi, k, axis=0))
            hr, hi = hr + tr, hi + ti
        cr = jnp.broadcast_to(carry_ref[0:1, cs], hr.shape)
        cim = jnp.broadcast_to(carry_ref[1:2, cs], hi.shape)
        tr, ti = _cmul(coef_ref[6, :, cs], coef_ref[7, :, cs], cr, cim)
        hr, hi = hr + tr, hi + ti
        carry_ref[0:1, cs] = hr[SUBLANES - 1:SUBLANES, :]
        carry_ref[1:2, cs] = hi[SUBLANES - 1:SUBLANES, :]
        gr = jnp.where(rid == 0, cr, pltpu.roll(hr, 1, axis=0))
        gi = jnp.where(rid == 0, cim, pltpu.roll(hi, 1, axis=0))

        for j in range(0, nj, 2):
            outs = []
            for jj in (j, j + 1):
                rows = slice(jj * SUBLANES, (jj + 1) * SUBLANES)
                gr, gi = _cmul(ar, ai, gr, gi)
                outs.append((bu_ref[rows, re] + gr, bu_ref[rows, im] + gi))
            rows2 = slice(j * SUBLANES, (j + 2) * SUBLANES)
            hb_ref[rows2, re] = jnp.concatenate([outs[0][0], outs[1][0]], axis=0).astype(BF16)
            hb_ref[rows2, im] = jnp.concatenate([outs[0][1], outs[1][1]], axis=0).astype(BF16)

        y_parts.append(jnp.dot(hb_ref[:, i * sg_cols:(i + 1) * sg_cols], csg_ref[i],
                               preferred_element_type=F32))

    for proj in fillers:
        proj()
    y = jnp.concatenate(y_parts, axis=-1)
    u = uzp[:, :ssm_w].astype(F32)
    z = uzp[:, ssm_w:].astype(F32)
    y = jax.nn.gelu(y + d_ref[...] * u)
    gate = jnp.dot(y.astype(BF16), wglu_ref[...], preferred_element_type=F32) + bglu_ref[...]
    y = y * jax.nn.sigmoid(gate)
    y = y * (z * jax.nn.sigmoid(z))
    ysp = _rms(y, gs_ref[...]).astype(BF16)
    ys_ref[...] = jnp.dot(perm_ref[1], ysp, preferred_element_type=F32).astype(BF16)


def _attn_kernel(qt_ref, k_ref, aug_ref, vt_ref, o_ref, qc_ref, sa_ref, sb_ref, sc_ref, sd_ref,
                 ta_ref, tb_ref, tc_ref, td_ref, m_ref, acc_ref, *, tq, tk, td):
    qi = pl.program_id(2)
    n_h = LANES // HEAD_DIM

    zero = jnp.zeros((HEAD_DIM, tq), BF16)
    for h in range(n_h):
        for hh in range(n_h):
            qc_ref[h * HEAD_DIM:(h + 1) * HEAD_DIM, hh * tq:(hh + 1) * tq] = (
                qt_ref[0, h * HEAD_DIM:(h + 1) * HEAD_DIM, :] if h == hh else zero)
    r = lax.broadcasted_iota(jnp.int32, (LANES, n_h * tq), 0)
    c = lax.broadcasted_iota(jnp.int32, (LANES, n_h * tq), 1)
    ones = (r >= 0) & (r < 3) & (c < tq)
    for h in range(1, n_h):
        ones = ones | ((r >= 3 * h) & (r < 3 * h + 3) & (c >= h * tq) & (c < (h + 1) * tq))
    qc_ref[LANES:, :] = jnp.where(ones, 1.0, 0.0).astype(BF16)

    def pv(pb, start, h):
        size = pb.shape[0]
        lhs = jnp.concatenate([vt_ref[0, h * HEAD_DIM:(h + 1) * HEAD_DIM, pl.ds(start, size)],
                               jnp.ones((PV_ONES_ROWS, size), BF16)], axis=0)
        return jnp.dot(lhs, pb, preferred_element_type=F32)

    n_pairs = qi * (tq // (2 * tk))

    def key_rows(start, size=tk):
        return jnp.concatenate([k_ref[0, pl.ds(start, size), :],
                                aug_ref[0, 0, pl.ds(start, size), :]], axis=1)

    def scores(lhs, cols, s_ref, t_ref):
        s = jnp.dot(lhs, qc_ref[:, cols], preferred_element_type=F32)
        s_ref[:, cols] = s
        t_ref[:, cols] = jnp.max(s, axis=0, keepdims=True)

    def softmax_pv(start, cols, s_ref, t_ref):
        h = cols.start // tq
        qcols = slice(cols.start - h * tq, cols.stop - h * tq)
        m_old = m_ref[:, cols]
        m_new = jnp.maximum(m_old, t_ref[:, cols])
        alpha = jnp.exp2(m_old - m_new)
        pb = jnp.exp2(s_ref[:, cols] - m_new).astype(BF16)
        m_ref[:, cols] = m_new
        acc_ref[h, :, qcols] = alpha * acc_ref[h, :, qcols] + pv(pb, start, h)

    sets = (((sa_ref, ta_ref), (sb_ref, tb_ref)), ((sc_ref, tc_ref), (sd_ref, td_ref)))
    col_blocks = [slice(c, c + QCOL_BLOCK) for c in range(0, n_h * tq, QCOL_BLOCK)]

    def pair_body(p, cur, nxt):
        base = pl.multiple_of(p * (2 * tk), 2 * tk)
        for t in range(2):
            start = base + t * tk
            if nxt is not None:
                lhs = key_rows(start + 2 * tk)
            for cols in col_blocks:
                if nxt is not None:
                    scores(lhs, cols, *nxt[t])
                softmax_pv(start, cols, *cur[t])

    n_sub = tq // td
    d0 = pl.multiple_of(qi * tq, tq)
    kk = lax.broadcasted_iota(jnp.int32, (td, td), 0)
    qq = lax.broadcasted_iota(jnp.int32, (td, td), 1)
    causal = jnp.where(kk <= qq, 0.0, NEG_BIG)
    blocks = [(h, c) for c in range(n_sub) for h in range(n_h)]
    sq = {}
    for r in range(n_sub):
        lhs = key_rows(d0 + r * td, td)
        for h, c in blocks:
            if c >= r:
                cols = slice(h * tq + c * td, h * tq + (c + 1) * td)
                s = jnp.dot(lhs, qc_ref[:, cols], preferred_element_type=F32)
                sq[h, c, r] = s + causal if r == c else s
    first_pair = [(t, cols) for t in range(2) for cols in col_blocks]
    first_lhs = [key_rows(t * tk) for t in range(2)]
    state = {}
    n_done = 0
    for r in range(n_sub):
        for h, c in blocks:
            if c < r:
                continue
            s = sq[h, c, r]
            t = jnp.max(s, axis=0, keepdims=True)
            if r == 0:
                m_new = t
            else:
                m_old, acc = state[h, c]
                m_new = jnp.maximum(m_old, t)
                alpha = jnp.exp2(m_old - m_new)
            upd = pv(jnp.exp2(s - m_new).astype(BF16), d0 + r * td, h)
            acc = upd if r == 0 else alpha * acc + upd
            state[h, c] = (m_new, acc)
            if r == c:
                m_ref[:, h * tq + c * td:h * tq + (c + 1) * td] = m_new
                acc_ref[h, :, c * td:(c + 1) * td] = acc
            n_done += 1
            if n_done % 2 == 0 and first_pair:
                t0, cols0 = first_pair.pop(0)
                scores(first_lhs[t0], cols0, *sets[0][t0])
    for t0, cols0 in first_pair:
        scores(first_lhs[t0], cols0, *sets[0][t0])

    def pair(p, carry):
        for parity in range(2):
            @pl.when(p % 2 == parity)
            def _(parity=parity):
                pair_body(p, sets[parity], sets[1 - parity])
        return carry

    lax.fori_loop(0, n_pairs - 1, pair, 0)

    for parity in range(2):
        @pl.when((n_pairs > 0) & ((n_pairs - 1) % 2 == parity))
        def _(parity=parity):
            pair_body(n_pairs - 1, sets[parity], None)

    ot = jnp.concatenate(
        [acc_ref[h, :HEAD_DIM, :] * (1.0 / acc_ref[h, HEAD_DIM:HEAD_DIM + 1, :])
         for h in range(n_h)], axis=0)
    o_ref[0] = ot.T.astype(BF16)


def _out_proj_kernel(ys_ref, oa_ref, za_ref, ga_ref, w_ref, x_ref, gp_ref, out_ref, *, ssm_w):
    z = za_ref[...].astype(F32)
    ya = _rms(oa_ref[...].astype(F32) * (z * jax.nn.sigmoid(z)), ga_ref[...]).astype(BF16)
    y = (jnp.dot(ys_ref[...], w_ref[:ssm_w, :], preferred_element_type=F32)
         + jnp.dot(ya, w_ref[ssm_w:, :], preferred_element_type=F32))
    out_ref[...] = x_ref[...] + _rms(y, gp_ref[...])


def _s5_params(a_re, a_im, log_dt, b_re, b_im, c_re, c_im, nj):
    g, n = a_re.shape
    h = b_re.shape[-1]
    n_sg = g // SG_GROUPS
    dt = jnp.exp(log_dt)[:, None]
    mag, ang = jnp.exp(dt * a_re), dt * a_im
    abar_re, abar_im = mag * jnp.cos(ang), mag * jnp.sin(ang)
    den = a_re * a_re + a_im * a_im
    nr, ni = abar_re - 1.0, abar_im
    coef_re = (nr * a_re + ni * a_im) / den
    coef_im = (ni * a_re - nr * a_im) / den
    bbar_re = coef_re[..., None] * b_re - coef_im[..., None] * b_im
    bbar_im = coef_re[..., None] * b_im + coef_im[..., None] * b_re
    eye = jnp.eye(SG_GROUPS, dtype=F32)
    bb = jnp.stack([bbar_re, bbar_im]).reshape(2, n_sg, SG_GROUPS, n, h)
    bsg = jnp.einsum('pignh,gk->ighpkn', bb, eye).reshape(n_sg, SG_GROUPS * h, 2 * SG_GROUPS * n)
    cc = jnp.stack([c_re, -c_im]).reshape(2, n_sg, SG_GROUPS, h, n)
    csg = jnp.einsum('pighn,gk->ipgnkh', cc, eye).reshape(n_sg, 2 * SG_GROUPS * n, SG_GROUPS * h)

    def apow(m):
        m = jnp.asarray(m, F32)[:, None, None]
        pm, pa = jnp.exp(m * (dt * a_re)), m * ang
        return (pm * jnp.cos(pa)).reshape(-1, g * n), (pm * jnp.sin(pa)).reshape(-1, g * n)

    rows = jnp.arange(SUBLANES)
    coef = []
    for k in (1, 2, 4):
        pr, pi = apow([nj * k])
        live = (rows >= k)[:, None]
        coef += [jnp.where(live, pr, 0.0), jnp.where(live, pi, 0.0)]
    pr, pi = apow(nj * (rows + 1))
    coef += [pr, pi]
    pr, pi = apow(jnp.ones((SUBLANES,)))
    coef += [pr, pi]
    return bsg.astype(BF16), csg.astype(BF16), jnp.stack(coef)


def _layer(x, norm_pre_g, w_in, b_forget, a_re, a_im, log_dt, b_re, b_im, c_re, c_im, d_skip,
           w_glu, b_glu, g_ssm, g_attn, w_out, norm_post_g):
    bsz, seq, d_model = x.shape
    n_heads = b_forget.shape[0]
    attn_w = n_heads * HEAD_DIM
    ssm_w = d_skip.shape[0]
    tokens = bsz * seq
    tm, to, tq = TM_PROJ, TM_OUT, TQ_ATTN
    assert seq % tm == 0 and tokens % to == 0 and seq % tq == 0
    assert ssm_w % LANES == 0 and attn_w % LANES == 0 and n_heads <= 2 * SUBLANES
    nt = seq // tm
    x2 = x.reshape(tokens, d_model)
    row = lambda v: v.reshape(1, -1).astype(F32)

    c0 = 2 * ssm_w
    w_q, w_k, w_v, w_za, w_f = (w_in[:, c0 + j * attn_w:c0 + (j + 1) * attn_w] for j in range(5))
    w_main = jnp.concatenate([w_in[:, :c0], w_k, w_za], axis=1).astype(BF16)
    w_t = jnp.concatenate([w_q.T, w_v.T, w_f.T,
                           jnp.zeros((2 * SUBLANES - n_heads, d_model), F32)], axis=0).astype(BF16)

    assert tm % (2 * SUBLANES * SUBLANES) == 0 and (tm // SUBLANES) & (tm // SUBLANES - 1) == 0
    bsg, csg, coef = _s5_params(a_re, a_im, log_dt, b_re, b_im, c_re, c_im, tm // SUBLANES)
    n_state = a_re.size
    src_row = (jnp.arange(tm) % SUBLANES) * (tm // SUBLANES) + jnp.arange(tm) // SUBLANES
    perm = (src_row[:, None] == jnp.arange(tm)[None, :])
    perms = jnp.stack([perm, perm.T]).astype(BF16)
    const2 = lambda i: (0, 0)
    const3 = lambda i: (0, 0, 0)
    time_major = lambda i: (i // nt, 0, i % nt)

    ys, k, za, qt, vt, ft = pl.pallas_call(
        functools.partial(_in_proj_s5_kernel, ssm_w=ssm_w, attn_w=attn_w, n_heads=n_heads, tl=tm,
                          tiles_per_seq=nt),
        grid=(tokens // tm,),
        in_specs=[pl.BlockSpec((tm, d_model), lambda i: (i, 0)),
                  pl.BlockSpec((1, d_model), const2),
                  pl.BlockSpec(w_main.shape, const2),
                  pl.BlockSpec(w_t.shape, const2),
                  pl.BlockSpec(perms.shape, const3),
                  pl.BlockSpec(bsg.shape, const3),
                  pl.BlockSpec(csg.shape, const3),
                  pl.BlockSpec(coef.shape, const3),
                  pl.BlockSpec((1, ssm_w), const2),
                  pl.BlockSpec((ssm_w, ssm_w), const2),
                  pl.BlockSpec((1, ssm_w), const2),
                  pl.BlockSpec((1, ssm_w), const2)],
        out_specs=[pl.BlockSpec((tm, ssm_w), lambda i: (i, 0)),
                   pl.BlockSpec((tm, attn_w), lambda i: (i, 0)),
                   pl.BlockSpec((tm, attn_w), lambda i: (i, 0)),
                   pl.BlockSpec((1, attn_w, tm), time_major),
                   pl.BlockSpec((1, attn_w, tm), time_major),
                   pl.BlockSpec((1, n_heads, tm), time_major)],
        out_shape=[jax.ShapeDtypeStruct((tokens, ssm_w), BF16),
                   jax.ShapeDtypeStruct((tokens, attn_w), BF16),
                   jax.ShapeDtypeStruct((tokens, attn_w), BF16),
                   jax.ShapeDtypeStruct((bsz, attn_w, seq), BF16),
                   jax.ShapeDtypeStruct((bsz, attn_w, seq), BF16),
                   jax.ShapeDtypeStruct((bsz, n_heads, seq), F32)],
        scratch_shapes=[pltpu.VMEM((tm, 2 * n_state), F32),
                        pltpu.VMEM((tm, 2 * n_state), BF16),
                        pltpu.VMEM((2, n_state), F32)],
        compiler_params=_cparams("arbitrary"),
        name="in_proj_s5",
    )(x2, row(norm_pre_g), w_main, w_t, perms, bsg, csg, coef, row(d_skip), w_glu.astype(BF16),
      row(b_glu), row(g_ssm))

    n_hp = n_heads // (LANES // HEAD_DIM)
    aug = pl.pallas_call(
        functools.partial(_forget_cum_kernel, seq=seq, n_hp=n_hp, chunk=CUM_CHUNK),
        grid=(bsz,),
        in_specs=[pl.BlockSpec((1, n_heads, seq), lambda b: (b, 0, 0)),
                  pl.BlockSpec((n_heads, 1), lambda b: (0, 0))],
        out_specs=pl.BlockSpec((1, n_hp, seq, LANES), lambda b: (b, 0, 0, 0)),
        out_shape=jax.ShapeDtypeStruct((bsz, n_hp, seq, LANES), BF16),
        compiler_params=_cparams("arbitrary"),
        name="forget_cum",
    )(ft, b_forget.reshape(n_heads, 1).astype(F32))

    tk, td = TK_ATTN, TD_ATTN
    assert tq % (2 * tk) == 0 and tq % td == 0 and 3 * (LANES // HEAD_DIM) <= LANES
    nq = seq // tq
    wq = (LANES // HEAD_DIM) * tq
    oa = pl.pallas_call(
        functools.partial(_attn_kernel, tq=tq, tk=tk, td=td),
        grid=(bsz, n_hp, nq),
        in_specs=[pl.BlockSpec((1, LANES, tq), lambda b, p, i: (b, p, i)),
                  pl.BlockSpec((1, seq, LANES), lambda b, p, i: (b, 0, p)),
                  pl.BlockSpec((1, 1, seq, LANES), lambda b, p, i: (b, p, 0, 0)),
                  pl.BlockSpec((1, LANES, seq), lambda b, p, i: (b, p, 0))],
        out_specs=pl.BlockSpec((1, tq, LANES), lambda b, p, i: (b, i, p)),
        out_shape=jax.ShapeDtypeStruct((bsz, seq, attn_w), BF16),
        scratch_shapes=[pltpu.VMEM((2 * LANES, wq), BF16),
                        pltpu.VMEM((tk, wq + LANES), F32),
                        pltpu.VMEM((tk, wq + LANES), F32),
                        pltpu.VMEM((tk, wq + LANES), F32),
                        pltpu.VMEM((tk, wq + LANES), F32),
                        pltpu.VMEM((1, wq), F32),
                        pltpu.VMEM((1, wq), F32),
                        pltpu.VMEM((1, wq), F32),
                        pltpu.VMEM((1, wq), F32),
                        pltpu.VMEM((1, wq), F32),
                        pltpu.VMEM((LANES // HEAD_DIM, HEAD_DIM + PV_ONES_ROWS, tq), F32)],
        compiler_params=_cparams("arbitrary", "arbitrary", "arbitrary"),
        name="fox_attn",
    )(qt, k.reshape(bsz, seq, attn_w), aug, vt)

    out = pl.pallas_call(
        functools.partial(_out_proj_kernel, ssm_w=ssm_w),
        grid=(tokens // to,),
        in_specs=[pl.BlockSpec((to, ssm_w), lambda i: (i, 0)),
                  pl.BlockSpec((to, attn_w), lambda i: (i, 0)),
                  pl.BlockSpec((to, attn_w), lambda i: (i, 0)),
                  pl.BlockSpec((1, attn_w), lambda i: (0, 0)),
                  pl.BlockSpec((ssm_w + attn_w, d_model), lambda i: (0, 0)),
                  pl.BlockSpec((to, d_model), lambda i: (i, 0)),
                  pl.BlockSpec((1, d_model), lambda i: (0, 0))],
        out_specs=pl.BlockSpec((to, d_model), lambda i: (i, 0)),
        out_shape=jax.ShapeDtypeStruct((tokens, d_model), F32),
        compiler_params=_cparams("arbitrary"),
        name="out_proj",
    )(ys, oa.reshape(tokens, attn_w), za, row(g_attn), w_out.astype(BF16), x2, row(norm_post_g))
    return out.reshape(bsz, seq, d_model)


def kernel(x, norm_pre_g, w_in, b_forget, ssm_a_re, ssm_a_im, ssm_log_dt, ssm_b_re, ssm_b_im,
           ssm_c_re, ssm_c_im, ssm_d, w_glu, b_glu, g_ssm, g_attn, w_out, norm_post_g):
    h = x
    for layer in range(norm_pre_g.shape[0]):
        h = _layer(h, norm_pre_g[layer], w_in[layer], b_forget[layer], ssm_a_re[layer],
                   ssm_a_im[layer], ssm_log_dt[layer], ssm_b_re[layer], ssm_b_im[layer],
                   ssm_c_re[layer], ssm_c_im[layer], ssm_d[layer], w_glu[layer], b_glu[layer],
                   g_ssm[layer], g_attn[layer], w_out[layer], norm_post_g[layer])
    return h
```

```python
import functools
import math

import jax
import jax.numpy as jnp
from jax import lax
from jax.experimental import pallas as pl
from jax.experimental.pallas import tpu as pltpu

F32 = jnp.float32
BF16 = jnp.bfloat16

SSM_GROUP_CH = 16
SSM_STATE = 64
HEAD_DIM = 64
RMS_EPS = 1e-6

LANES = 128
SUBLANES = 8
VMEM_LIMIT_BYTES = 56 * 1024 * 1024

TM_PROJ = 512
TM_OUT = 1024
TL_SSM = 256
TQ_ATTN = 1024
TK_ATTN = 512
TD_ATTN = 256
QCOL_BLOCK = 512
CUM_CHUNK = 128
PV_ONES_ROWS = 2 * SUBLANES
LOG2E = math.log2(math.e)
NT_DIMS = (((1,), (1,)), ((), ()))
SG_GROUPS = LANES // SSM_GROUP_CH
SG_STATE = SG_GROUPS * SSM_STATE
SCAN_W = 256
NEG_BIG = -1e30
JUMP_LIMIT = 96.0


def _cparams(*sem):
    return pltpu.CompilerParams(dimension_semantics=sem, vmem_limit_bytes=VMEM_LIMIT_BYTES)


def _rms(v, g):
    return v * lax.rsqrt(jnp.mean(v * v, axis=-1, keepdims=True) + RMS_EPS) * g


def _forget_cum_kernel(ft_ref, bf_ref, aug_ref, *, seq, n_hp, chunk):
    f = ft_ref[0] + bf_ref[...]
    lf = jnp.minimum(f, 0.0) - jnp.log1p(jnp.exp(-jnp.abs(f)))
    lane = lax.broadcasted_iota(jnp.int32, lf.shape, 1)
    k = 1
    while k < seq:
        lf = lf + jnp.where(lane >= k, pltpu.roll(lf, k, axis=1), 0.0)
        k *= 2
    nb = lf * (-LOG2E)
    hi = nb.astype(BF16).astype(F32)
    mid = (nb - hi).astype(BF16).astype(F32)
    lo = (nb - hi - mid).astype(BF16).astype(F32)
    parts = (hi, mid, lo)
    rid = lax.broadcasted_iota(jnp.int32, (SUBLANES, seq), 0)
    eye = (lax.broadcasted_iota(jnp.int32, (chunk, chunk), 0)
           == lax.broadcasted_iota(jnp.int32, (chunk, chunk), 1)).astype(BF16)
    pad = jnp.zeros((LANES - SUBLANES, seq), F32)
    for p in range(n_hp):
        rows = jnp.zeros((SUBLANES, seq), F32)
        for j in range(LANES // HEAD_DIM):
            for t, part in enumerate(parts):
                src = part[2 * p + j:2 * p + j + 1, :]
                rows = jnp.where(rid == 3 * j + t, jnp.broadcast_to(src, rows.shape), rows)
        bmat = jnp.concatenate([rows, pad], axis=0).astype(BF16)
        for c in range(seq // chunk):
            blk = lax.dot_general(eye, bmat[:, c * chunk:(c + 1) * chunk], NT_DIMS,
                                  preferred_element_type=F32)
            aug_ref[0, p, c * chunk:(c + 1) * chunk, :] = blk.astype(BF16)


def _cmul(ar, ai, xr, xi):
    return ar * xr - ai * xi, ar * xi + ai * xr


def _in_proj_s5_kernel(x_ref, g_ref, w_ref, wt_ref, perm_ref, bsg_ref, csg_ref, coef_ref, d_ref,
                       wglu_ref, bglu_ref, gs_ref, ys_ref, k_ref, za_ref, qt_ref, vt_ref, ft_ref,
                       bu_ref, hb_ref, carry_ref, *, ssm_w, attn_w, n_heads, tl, tiles_per_seq):
    n_sg = ssm_w // LANES
    sg_cols = 2 * SG_STATE
    nj = tl // SUBLANES
    c0 = 2 * ssm_w

    @pl.when(pl.program_id(0) % tiles_per_seq == 0)
    def _():
        carry_ref[...] = jnp.zeros_like(carry_ref)

    xb = _rms(x_ref[...], g_ref[...]).astype(BF16)
    uz = jnp.dot(xb, w_ref[:, :c0], preferred_element_type=F32).astype(BF16)
    uzp = jnp.dot(perm_ref[0], uz, preferred_element_type=F32).astype(BF16)

    def proj_k():
        k_ref[...] = jnp.dot(xb, w_ref[:, c0:c0 + attn_w],
                             preferred_element_type=F32).astype(BF16)

    def proj_za():
        za_ref[...] = jnp.dot(xb, w_ref[:, c0 + attn_w:], preferred_element_type=F32).astype(BF16)

    def proj_qt():
        qt = lax.dot_general(wt_ref[:attn_w], xb, NT_DIMS, preferred_element_type=F32)
        qt_ref[0] = (qt * (HEAD_DIM ** -0.5 * LOG2E)).astype(BF16)

    def proj_vt():
        vt_ref[0] = lax.dot_general(wt_ref[attn_w:2 * attn_w], xb, NT_DIMS,
                                    preferred_element_type=F32).astype(BF16)
        ft = lax.dot_general(wt_ref[2 * attn_w:], xb, NT_DIMS, preferred_element_type=F32)
        ft_ref[0] = ft[:n_heads]

    fillers = [proj_k, proj_za, proj_qt, proj_vt]

    for i in range(n_sg):
        bu_ref[:, i * sg_cols:(i + 1) * sg_cols] = jnp.dot(
            uzp[:, i * LANES:(i + 1) * LANES], bsg_ref[i], preferred_element_type=F32)

    rid = lax.broadcasted_iota(jnp.int32, (SUBLANES, SG_STATE), 0)
    y_parts = []
    for i in range(n_sg):
        re = slice(i * sg_cols, i * sg_cols + SG_STATE)
        im = slice(i * sg_cols + SG_STATE, (i + 1) * sg_cols)
        cs = slice(i * SG_STATE, (i + 1) * SG_STATE)
        if fillers:
            fillers.pop(0)()
        ar, ai = coef_ref[8, :, cs], coef_ref[9, :, cs]

        hr = jnp.zeros((SUBLANES, SG_STATE), F32)
        hi = jnp.zeros((SUBLANES, SG_STATE), F32)
        for j in range(nj):
            rows = slice(j * SUBLANES, (j + 1) * SUBLANES)
            tr, ti = _cmul(ar, ai, hr, hi)
            hr, hi = tr + bu_ref[rows, re], ti + bu_ref[rows, im]
            bu_ref[rows, re] = hr
            bu_ref[rows, im] = hi

        for t in range(3):
            k = 1 << t
            tr, ti = _cmul(coef_ref[2 * t, :, cs], coef_ref[2 * t + 1, :, cs],
                           pltpu.roll(hr, k, axis=0), pltpu.roll(hi, k, axis=0))
            hr, hi = hr + tr, hi + ti
        cr = jnp.broadcast_to(carry_ref[0:1, cs], hr.shape)
        cim = jnp.broadcast_to(carry_ref[1:2, cs], hi.shape)
        tr, ti = _cmul(coef_ref[6, :, cs], coef_ref[7, :, cs], cr, cim)
        hr, hi = hr + tr, hi + ti
        carry_ref[0:1, cs] = hr[SUBLANES - 1:SUBLANES, :]
        carry_ref[1:2, cs] = hi[SUBLANES - 1:SUBLANES, :]
        gr = jnp.where(rid == 0, cr, pltpu.roll(hr, 1, axis=0))
        gi = jnp.where(rid == 0, cim, pltpu.roll(hi, 1, axis=0))

        for j in range(0, nj, 2):
            outs = []
            for jj in (j, j + 1):
                rows = slice(jj * SUBLANES, (jj + 1) * SUBLANES)
                gr, gi = _cmul(ar, ai, gr, gi)
                outs.append((bu_ref[rows, re] + gr, bu_ref[rows, im] + gi))
            rows2 = slice(j * SUBLANES, (j + 2) * SUBLANES)
            hb_ref[rows2, re] = jnp.concatenate([outs[0][0], outs[1][0]], axis=0).astype(BF16)
            hb_ref[rows2, im] = jnp.concatenate([outs[0][1], outs[1][1]], axis=0).astype(BF16)

        y_parts.append(jnp.dot(hb_ref[:, i * sg_cols:(i + 1) * sg_cols], csg_ref[i],
                               preferred_element_type=F32))

    for proj in fillers:
        proj()
    y = jnp.concatenate(y_parts, axis=-1)
    u = uzp[:, :ssm_w].astype(F32)
    z = uzp[:, ssm_w:].astype(F32)
    y = jax.nn.gelu(y + d_ref[...] * u)
    gate = jnp.dot(y.astype(BF16), wglu_ref[...], preferred_element_type=F32) + bglu_ref[...]
    y = y * jax.nn.sigmoid(gate)
    y = y * (z * jax.nn.sigmoid(z))
    ysp = _rms(y, gs_ref[...]).astype(BF16)
    ys_ref[...] = jnp.dot(perm_ref[1], ysp, preferred_element_type=F32).astype(BF16)


def _attn_kernel(qt_ref, k_ref, aug_ref, vt_ref, o_ref, qc_ref, sa_ref, sb_ref, sc_ref, sd_ref,
                 ta_ref, tb_ref, tc_ref, td_ref, m_ref, acc_ref, m0_ref, acc0_ref, pend_ref,
                 pal_ref, jump_ref, *, tq, tk, td):
    qi = pl.program_id(2)
    n_h = LANES // HEAD_DIM

    zero = jnp.zeros((HEAD_DIM, tq), BF16)
    for h in range(n_h):
        for hh in range(n_h):
            qc_ref[h * HEAD_DIM:(h + 1) * HEAD_DIM, hh * tq:(hh + 1) * tq] = (
                qt_ref[0, h * HEAD_DIM:(h + 1) * HEAD_DIM, :] if h == hh else zero)
    r = lax.broadcasted_iota(jnp.int32, (LANES, n_h * tq), 0)
    c = lax.broadcasted_iota(jnp.int32, (LANES, n_h * tq), 1)
    ones = (r >= 0) & (r < 3) & (c < tq)
    for h in range(1, n_h):
        ones = ones | ((r >= 3 * h) & (r < 3 * h + 3) & (c >= h * tq) & (c < (h + 1) * tq))
    qc_ref[LANES:, :] = jnp.where(ones, 1.0, 0.0).astype(BF16)

    def pv(pb, start, h):
        size = pb.shape[0]
        lhs = jnp.concatenate([vt_ref[0, h * HEAD_DIM:(h + 1) * HEAD_DIM, pl.ds(start, size)],
                               jnp.ones((PV_ONES_ROWS, size), BF16)], axis=0)
        return jnp.dot(lhs, pb, preferred_element_type=F32)

    n_pairs = qi * (tq // (2 * tk))

    def key_rows(start, size=tk):
        return jnp.concatenate([k_ref[0, pl.ds(start, size), :],
                                aug_ref[0, 0, pl.ds(start, size), :]], axis=1)

    def scores(lhs, cols, s_ref, t_ref):
        s = jnp.dot(lhs, qc_ref[:, cols], preferred_element_type=F32)
        s_ref[:, cols] = s
        t_ref[:, cols] = jnp.max(s, axis=0, keepdims=True)

    def softmax_pv(start, cols, s_ref, t_ref):
        h = cols.start // tq
        qcols = slice(cols.start - h * tq, cols.stop - h * tq)
        m_old = m_ref[:, cols]
        m_new = jnp.maximum(m_old, t_ref[:, cols])
        alpha = jnp.exp2(m_old - m_new)
        pb = jnp.exp2(s_ref[:, cols] - m_new).astype(BF16)
        m_ref[:, cols] = m_new
        acc_ref[h, :, qcols] = alpha * acc_ref[h, :, qcols] + pv(pb, start, h)

    sets = (((sa_ref, ta_ref), (sb_ref, tb_ref)), ((sc_ref, tc_ref), (sd_ref, td_ref)))
    col_blocks = [slice(c, c + QCOL_BLOCK) for c in range(0, n_h * tq, QCOL_BLOCK)]

    def pair_body(p, cur, nxt):
        base = pl.multiple_of(p * (2 * tk), 2 * tk)
        for t in range(2):
            start = base + t * tk
            if nxt is not None:
                lhs = key_rows(start + 2 * tk)
            for cols in col_blocks:
                if nxt is not None:
                    scores(lhs, cols, *nxt[t])
                softmax_pv(start, cols, *cur[t])

    n_sub = tq // td
    d0 = pl.multiple_of(qi * tq, tq)
    kk = lax.broadcasted_iota(jnp.int32, (td, td), 0)
    qq = lax.broadcasted_iota(jnp.int32, (td, td), 1)
    causal = jnp.where(kk <= qq, 0.0, NEG_BIG)
    blocks = [(h, c) for c in range(n_sub) for h in range(n_h)]
    sq = {}
    for r in range(n_sub):
        lhs = key_rows(d0 + r * td, td)
        for h, c in blocks:
            if c >= r:
                cols = slice(h * tq + c * td, h * tq + (c + 1) * td)
                s = jnp.dot(lhs, qc_ref[:, cols], preferred_element_type=F32)
                sq[h, c, r] = s + causal if r == c else s
    state = {}
    for r in range(n_sub):
        for h, c in blocks:
            if c < r:
                continue
            s = sq[h, c, r]
            t = jnp.max(s, axis=0, keepdims=True)
            if r == 0:
                m_new = t
            else:
                m_old, acc = state[h, c]
                m_new = jnp.maximum(m_old, t)
                alpha = jnp.exp2(m_old - m_new)
            upd = pv(jnp.exp2(s - m_new).astype(BF16), d0 + r * td, h)
            acc = upd if r == 0 else alpha * acc + upd
            state[h, c] = (m_new, acc)
            if r == c:
                dcols = slice(h * tq + c * td, h * tq + (c + 1) * td)
                m_ref[:, dcols] = m_new
                m0_ref[:, dcols] = m_new
                acc_ref[h, :, c * td:(c + 1) * td] = acc
                acc0_ref[h, :, c * td:(c + 1) * td] = acc

    n_off = qi * (tq // tk)
    jump_ref[...] = jnp.zeros_like(jump_ref)
    pend_ref[...] = jnp.zeros_like(pend_ref)
    pal_ref[...] = jnp.ones_like(pal_ref)

    def flush(pending):
        p, alpha, start, cols = pending
        h = cols.start // tq
        qcols = slice(cols.start - h * tq, cols.stop - h * tq)
        acc_ref[h, :, qcols] = (acc_ref[h, :, qcols] + pv(p, start, h)) * alpha

    def fast_tile(start, pending):
        lhs = key_rows(start)
        for cols in col_blocks:
            s = jnp.dot(lhs, qc_ref[:, cols], preferred_element_type=F32)
            rho = m_ref[:, cols]
            t = jnp.max(s, axis=0, keepdims=True)
            p = jnp.exp2(s - rho).astype(BF16)
            flush(pending)
            m_new = jnp.maximum(rho, t)
            m_ref[:, cols] = m_new
            jump_ref[:, cols] = jnp.maximum(jump_ref[:, cols], t - rho)
            pending = (p, jnp.exp2(rho - m_new), start, cols)
        return pending

    def fast_body(d, carry):
        start_a = pl.multiple_of((n_off - 1 - 2 * d) * tk, tk)
        prev_start = pl.multiple_of(jnp.where(d > 0, start_a + tk, start_a), tk)
        pending = (pend_ref[...], pal_ref[...], prev_start, col_blocks[-1])
        pending = fast_tile(start_a, pending)
        pending = fast_tile(start_a - tk, pending)
        pend_ref[...] = pending[0]
        pal_ref[...] = pending[1]
        return carry

    lax.fori_loop(0, n_off // 2, fast_body, 0)
    flush((pend_ref[...], pal_ref[...], 0, col_blocks[-1]))

    @pl.when(jnp.max(jump_ref[...]) > JUMP_LIMIT)
    def _():
        m_ref[...] = m0_ref[...]
        acc_ref[...] = acc0_ref[...]
        for t in range(2):
            lhs = key_rows(t * tk)
            for cols in col_blocks:
                scores(lhs, cols, *sets[0][t])

        def pair(p, carry):
            for parity in range(2):
                @pl.when(p % 2 == parity)
                def _(parity=parity):
                    pair_body(p, sets[parity], sets[1 - parity])
            return carry

        lax.fori_loop(0, n_pairs - 1, pair, 0)
        for parity in range(2):
            @pl.when((n_pairs > 0) & ((n_pairs - 1) % 2 == parity))
            def _(parity=parity):
                pair_body(n_pairs - 1, sets[parity], None)

    ot = jnp.concatenate(
        [acc_ref[h, :HEAD_DIM, :] * (1.0 / acc_ref[h, HEAD_DIM:HEAD_DIM + 1, :])
         for h in range(n_h)], axis=0)
    o_ref[0] = ot.T.astype(BF16)


def _out_proj_kernel(ys_ref, oa_ref, za_ref, ga_ref, w_ref, x_ref, gp_ref, out_ref, *, ssm_w):
    z = za_ref[...].astype(F32)
    ya = _rms(oa_ref[...].astype(F32) * (z * jax.nn.sigmoid(z)), ga_ref[...]).astype(BF16)
    y = (jnp.dot(ys_ref[...], w_ref[:ssm_w, :], preferred_element_type=F32)
         + jnp.dot(ya, w_ref[ssm_w:, :], preferred_element_type=F32))
    out_ref[...] = x_ref[...] + _rms(y, gp_ref[...])


def _s5_params(a_re, a_im, log_dt, b_re, b_im, c_re, c_im, nj):
    g, n = a_re.shape
    h = b_re.shape[-1]
    n_sg = g // SG_GROUPS
    dt = jnp.exp(log_dt)[:, None]
    mag, ang = jnp.exp(dt * a_re), dt * a_im
    abar_re, abar_im = mag * jnp.cos(ang), mag * jnp.sin(ang)
    den = a_re * a_re + a_im * a_im
    nr, ni = abar_re - 1.0, abar_im
    coef_re = (nr * a_re + ni * a_im) / den
    coef_im = (ni * a_re - nr * a_im) / den
    bbar_re = coef_re[..., None] * b_re - coef_im[..., None] * b_im
    bbar_im = coef_re[..., None] * b_im + coef_im[..., None] * b_re
    eye = jnp.eye(SG_GROUPS, dtype=F32)
    bb = jnp.stack([bbar_re, bbar_im]).reshape(2, n_sg, SG_GROUPS, n, h)
    bsg = jnp.einsum('pignh,gk->ighpkn', bb, eye).reshape(n_sg, SG_GROUPS * h, 2 * SG_GROUPS * n)
    cc = jnp.stack([c_re, -c_im]).reshape(2, n_sg, SG_GROUPS, h, n)
    csg = jnp.einsum('pighn,gk->ipgnkh', cc, eye).reshape(n_sg, 2 * SG_GROUPS * n, SG_GROUPS * h)

    def apow(m):
        m = jnp.asarray(m, F32)[:, None, None]
        pm, pa = jnp.exp(m * (dt * a_re)), m * ang
        return (pm * jnp.cos(pa)).reshape(-1, g * n), (pm * jnp.sin(pa)).reshape(-1, g * n)

    rows = jnp.arange(SUBLANES)
    coef = []
    for k in (1, 2, 4):
        pr, pi = apow([nj * k])
        live = (rows >= k)[:, None]
        coef += [jnp.where(live, pr, 0.0), jnp.where(live, pi, 0.0)]
    pr, pi = apow(nj * (rows + 1))
    coef += [pr, pi]
    pr, pi = apow(jnp.ones((SUBLANES,)))
    coef += [pr, pi]
    return bsg.astype(BF16), csg.astype(BF16), jnp.stack(coef)


def _layer(x, norm_pre_g, w_in, b_forget, a_re, a_im, log_dt, b_re, b_im, c_re, c_im, d_skip,
           w_glu, b_glu, g_ssm, g_attn, w_out, norm_post_g):
    bsz, seq, d_model = x.shape
    n_heads = b_forget.shape[0]
    attn_w = n_heads * HEAD_DIM
    ssm_w = d_skip.shape[0]
    tokens = bsz * seq
    tm, to, tq = TM_PROJ, TM_OUT, TQ_ATTN
    assert seq % tm == 0 and tokens % to == 0 and seq % tq == 0
    assert ssm_w % LANES == 0 and attn_w % LANES == 0 and n_heads <= 2 * SUBLANES
    nt = seq // tm
    x2 = x.reshape(tokens, d_model)
    row = lambda v: v.reshape(1, -1).astype(F32)

    c0 = 2 * ssm_w
    w_q, w_k, w_v, w_za, w_f = (w_in[:, c0 + j * attn_w:c0 + (j + 1) * attn_w] for j in range(5))
    w_main = jnp.concatenate([w_in[:, :c0], w_k, w_za], axis=1).astype(BF16)
    w_t = jnp.concatenate([w_q.T, w_v.T, w_f.T,
                           jnp.zeros((2 * SUBLANES - n_heads, d_model), F32)], axis=0).astype(BF16)

    assert tm % (2 * SUBLANES * SUBLANES) == 0 and (tm // SUBLANES) & (tm // SUBLANES - 1) == 0
    bsg, csg, coef = _s5_params(a_re, a_im, log_dt, b_re, b_im, c_re, c_im, tm // SUBLANES)
    n_state = a_re.size
    src_row = (jnp.arange(tm) % SUBLANES) * (tm // SUBLANES) + jnp.arange(tm) // SUBLANES
    perm = (src_row[:, None] == jnp.arange(tm)[None, :])
    perms = jnp.stack([perm, perm.T]).astype(BF16)
    const2 = lambda i: (0, 0)
    const3 = lambda i: (0, 0, 0)
    time_major = lambda i: (i // nt, 0, i % nt)

    ys, k, za, qt, vt, ft = pl.pallas_call(
        functools.partial(_in_proj_s5_kernel, ssm_w=ssm_w, attn_w=attn_w, n_heads=n_heads, tl=tm,
                          tiles_per_seq=nt),
        grid=(tokens // tm,),
        in_specs=[pl.BlockSpec((tm, d_model), lambda i: (i, 0)),
                  pl.BlockSpec((1, d_model), const2),
                  pl.BlockSpec(w_main.shape, const2),
                  pl.BlockSpec(w_t.shape, const2),
                  pl.BlockSpec(perms.shape, const3),
                  pl.BlockSpec(bsg.shape, const3),
                  pl.BlockSpec(csg.shape, const3),
                  pl.BlockSpec(coef.shape, const3),
                  pl.BlockSpec((1, ssm_w), const2),
                  pl.BlockSpec((ssm_w, ssm_w), const2),
                  pl.BlockSpec((1, ssm_w), const2),
                  pl.BlockSpec((1, ssm_w), const2)],
        out_specs=[pl.BlockSpec((tm, ssm_w), lambda i: (i, 0)),
                   pl.BlockSpec((tm, attn_w), lambda i: (i, 0)),
                   pl.BlockSpec((tm, attn_w), lambda i: (i, 0)),
                   pl.BlockSpec((1, attn_w, tm), time_major),
                   pl.BlockSpec((1, attn_w, tm), time_major),
                   pl.BlockSpec((1, n_heads, tm), time_major)],
        out_shape=[jax.ShapeDtypeStruct((tokens, ssm_w), BF16),
                   jax.ShapeDtypeStruct((tokens, attn_w), BF16),
                   jax.ShapeDtypeStruct((tokens, attn_w), BF16),
                   jax.ShapeDtypeStruct((bsz, attn_w, seq), BF16),
                   jax.ShapeDtypeStruct((bsz, attn_w, seq), BF16),
                   jax.ShapeDtypeStruct((bsz, n_heads, seq), F32)],
        scratch_shapes=[pltpu.VMEM((tm, 2 * n_state), F32),
                        pltpu.VMEM((tm, 2 * n_state), BF16),
                        pltpu.VMEM((2, n_state), F32)],
        compiler_params=_cparams("arbitrary"),
        name="in_proj_s5",
    )(x2, row(norm_pre_g), w_main, w_t, perms, bsg, csg, coef, row(d_skip), w_glu.astype(BF16),
      row(b_glu), row(g_ssm))

    n_hp = n_heads // (LANES // HEAD_DIM)
    aug = pl.pallas_call(
        functools.partial(_forget_cum_kernel, seq=seq, n_hp=n_hp, chunk=CUM_CHUNK),
        grid=(bsz,),
        in_specs=[pl.BlockSpec((1, n_heads, seq), lambda b: (b, 0, 0)),
                  pl.BlockSpec((n_heads, 1), lambda b: (0, 0))],
        out_specs=pl.BlockSpec((1, n_hp, seq, LANES), lambda b: (b, 0, 0, 0)),
        out_shape=jax.ShapeDtypeStruct((bsz, n_hp, seq, LANES), BF16),
        compiler_params=_cparams("arbitrary"),
        name="forget_cum",
    )(ft, b_forget.reshape(n_heads, 1).astype(F32))

    tk, td = TK_ATTN, TD_ATTN
    assert tq % (2 * tk) == 0 and tq % td == 0 and 3 * (LANES // HEAD_DIM) <= LANES
    nq = seq // tq
    wq = (LANES // HEAD_DIM) * tq
    oa = pl.pallas_call(
        functools.partial(_attn_kernel, tq=tq, tk=tk, td=td),
        grid=(bsz, n_hp, nq),
        in_specs=[pl.BlockSpec((1, LANES, tq), lambda b, p, i: (b, p, i)),
                  pl.BlockSpec((1, seq, LANES), lambda b, p, i: (b, 0, p)),
                  pl.BlockSpec((1, 1, seq, LANES), lambda b, p, i: (b, p, 0, 0)),
                  pl.BlockSpec((1, LANES, seq), lambda b, p, i: (b, p, 0))],
        out_specs=pl.BlockSpec((1, tq, LANES), lambda b, p, i: (b, i, p)),
        out_shape=jax.ShapeDtypeStruct((bsz, seq, attn_w), BF16),
        scratch_shapes=[pltpu.VMEM((2 * LANES, wq), BF16),
                        pltpu.VMEM((tk, wq + LANES), F32),
                        pltpu.VMEM((tk, wq + LANES), F32),
                        pltpu.VMEM((tk, wq + LANES), F32),
                        pltpu.VMEM((tk, wq + LANES), F32),
                        pltpu.VMEM((1, wq), F32),
                        pltpu.VMEM((1, wq), F32),
                        pltpu.VMEM((1, wq), F32),
                        pltpu.VMEM((1, wq), F32),
                        pltpu.VMEM((1, wq), F32),
                        pltpu.VMEM((LANES // HEAD_DIM, HEAD_DIM + PV_ONES_ROWS, tq), F32),
                        pltpu.VMEM((1, wq), F32),
                        pltpu.VMEM((LANES // HEAD_DIM, HEAD_DIM + PV_ONES_ROWS, tq), F32),
                        pltpu.VMEM((tk, QCOL_BLOCK), BF16),
                        pltpu.VMEM((1, QCOL_BLOCK), F32),
                        pltpu.VMEM((1, wq), F32)],
        compiler_params=_cparams("arbitrary", "arbitrary", "arbitrary"),
        name="fox_attn",
    )(qt, k.reshape(bsz, seq, attn_w), aug, vt)

    out = pl.pallas_call(
        functools.partial(_out_proj_kernel, ssm_w=ssm_w),
        grid=(tokens // to,),
        in_specs=[pl.BlockSpec((to, ssm_w), lambda i: (i, 0)),
                  pl.BlockSpec((to, attn_w), lambda i: (i, 0)),
                  pl.BlockSpec((to, attn_w), lambda i: (i, 0)),
                  pl.BlockSpec((1, attn_w), lambda i: (0, 0)),
                  pl.BlockSpec((ssm_w + attn_w, d_model), lambda i: (0, 0)),
                  pl.BlockSpec((to, d_model), lambda i: (i, 0)),
                  pl.BlockSpec((1, d_model), lambda i: (0, 0))],
        out_specs=pl.BlockSpec((to, d_model), lambda i: (i, 0)),
        out_shape=jax.ShapeDtypeStruct((tokens, d_model), F32),
        compiler_params=_cparams("arbitrary"),
        name="out_proj",
    )(ys, oa.reshape(tokens, attn_w), za, row(g_attn), w_out.astype(BF16), x2, row(norm_post_g))
    return out.reshape(bsz, seq, d_model)


def kernel(x, norm_pre_g, w_in, b_forget, ssm_a_re, ssm_a_im, ssm_log_dt, ssm_b_re, ssm_b_im,
           ssm_c_re, ssm_c_im, ssm_d, w_glu, b_glu, g_ssm, g_attn, w_out, norm_post_g):
    h = x
    for layer in range(norm_pre_g.shape[0]):
        h = _layer(h, norm_pre_g[layer], w_in[layer], b_forget[layer], ssm_a_re[layer],
                   ssm_a_im[layer], ssm_log_dt[layer], ssm_b_re[layer], ssm_b_im[layer],
                   ssm_c_re[layer], ssm_c_im[layer], ssm_d[layer], w_glu[layer], b_glu[layer],
                   g_ssm[layer], g_attn[layer], w_out[layer], norm_post_g[layer])
    return h
```

```python
import functools
import math

import jax
import jax.numpy as jnp
from jax import lax
from jax.experimental import pallas as pl
from jax.experimental.pallas import tpu as pltpu

F32 = jnp.float32
BF16 = jnp.bfloat16

SSM_GROUP_CH = 16
SSM_STATE = 64
HEAD_DIM = 64
RMS_EPS = 1e-6

LANES = 128
SUBLANES = 8
VMEM_LIMIT_BYTES = 56 * 1024 * 1024

TM_PROJ = 512
TM_OUT = 1024
TL_SSM = 256
TQ_ATTN = 1024
TK_ATTN = 512
TD_ATTN = 256
QCOL_BLOCK = 256
CUM_CHUNK = 128
PV_ONES_ROWS = 2 * SUBLANES
LOG2E = math.log2(math.e)
NT_DIMS = (((1,), (1,)), ((), ()))
SG_GROUPS = LANES // SSM_GROUP_CH
SG_STATE = SG_GROUPS * SSM_STATE
SCAN_W = 256
NEG_BIG = -1e30


def _cparams(*sem):
    return pltpu.CompilerParams(dimension_semantics=sem, vmem_limit_bytes=VMEM_LIMIT_BYTES)


def _rms(v, g):
    return v * lax.rsqrt(jnp.mean(v * v, axis=-1, keepdims=True) + RMS_EPS) * g


def _forget_cum_kernel(ft_ref, bf_ref, aug_ref, *, seq, n_hp, chunk):
    f = ft_ref[0] + bf_ref[...]
    lf = jnp.minimum(f, 0.0) - jnp.log1p(jnp.exp(-jnp.abs(f)))
    lane = lax.broadcasted_iota(jnp.int32, lf.shape, 1)
    k = 1
    while k < seq:
        lf = lf + jnp.where(lane >= k, pltpu.roll(lf, k, axis=1), 0.0)
        k *= 2
    nb = lf * (-LOG2E)
    hi = nb.astype(BF16).astype(F32)
    mid = (nb - hi).astype(BF16).astype(F32)
    lo = (nb - hi - mid).astype(BF16).astype(F32)
    parts = (hi, mid, lo)
    rid = lax.broadcasted_iota(jnp.int32, (SUBLANES, seq), 0)
    eye = (lax.broadcasted_iota(jnp.int32, (chunk, chunk), 0)
           == lax.broadcasted_iota(jnp.int32, (chunk, chunk), 1)).astype(BF16)
    pad = jnp.zeros((LANES - SUBLANES, seq), F32)
    for p in range(n_hp):
        rows = jnp.zeros((SUBLANES, seq), F32)
        for j in range(LANES // HEAD_DIM):
            for t, part in enumerate(parts):
                src = part[2 * p + j:2 * p + j + 1, :]
                rows = jnp.where(rid == 3 * j + t, jnp.broadcast_to(src, rows.shape), rows)
        bmat = jnp.concatenate([rows, pad], axis=0).astype(BF16)
        for c in range(seq // chunk):
            blk = lax.dot_general(eye, bmat[:, c * chunk:(c + 1) * chunk], NT_DIMS,
                                  preferred_element_type=F32)
            aug_ref[0, p, c * chunk:(c + 1) * chunk, :] = blk.astype(BF16)


def _cmul(ar, ai, xr, xi):
    return ar * xr - ai * xi, ar * xi + ai * xr


def _in_proj_s5_kernel(x_ref, g_ref, w_ref, wt_ref, perm_ref, bsg_ref, csg_ref, coef_ref, d_ref,
                       wglu_ref, bglu_ref, gs_ref, ys_ref, k_ref, za_ref, qt_ref, vt_ref, ft_ref,
                       bu_ref, hb_ref, carry_ref, *, ssm_w, attn_w, n_heads, tl, tiles_per_seq):
    n_sg = ssm_w // LANES
    sg_cols = 2 * SG_STATE
    nj = tl // SUBLANES
    c0 = 2 * ssm_w

    @pl.when(pl.program_id(0) % tiles_per_seq == 0)
    def _():
        carry_ref[...] = jnp.zeros_like(carry_ref)

    xb = _rms(x_ref[...], g_ref[...]).astype(BF16)
    uz = jnp.dot(xb, w_ref[:, :c0], preferred_element_type=F32).astype(BF16)
    uzp = jnp.dot(perm_ref[0], uz, preferred_element_type=F32).astype(BF16)

    def proj_k():
        k_ref[...] = jnp.dot(xb, w_ref[:, c0:c0 + attn_w],
                             preferred_element_type=F32).astype(BF16)

    def proj_za():
        za_ref[...] = jnp.dot(xb, w_ref[:, c0 + attn_w:], preferred_element_type=F32).astype(BF16)

    def proj_qt():
        qt = lax.dot_general(wt_ref[:attn_w], xb, NT_DIMS, preferred_element_type=F32)
        qt_ref[0] = (qt * (HEAD_DIM ** -0.5 * LOG2E)).astype(BF16)

    def proj_vt():
        vt_ref[0] = lax.dot_general(wt_ref[attn_w:2 * attn_w], xb, NT_DIMS,
                                    preferred_element_type=F32).astype(BF16)
        ft = lax.dot_general(wt_ref[2 * attn_w:], xb, NT_DIMS, preferred_element_type=F32)
        ft_ref[0] = ft[:n_heads]

    fillers = [proj_k, proj_za, proj_qt, proj_vt]

    for i in range(n_sg):
        bu_ref[:, i * sg_cols:(i + 1) * sg_cols] = jnp.dot(
            uzp[:, i * LANES:(i + 1) * LANES], bsg_ref[i], preferred_element_type=F32)

    rid = lax.broadcasted_iota(jnp.int32, (SUBLANES, SG_STATE), 0)
    y_parts = []
    for i in range(n_sg):
        re = slice(i * sg_cols, i * sg_cols + SG_STATE)
        im = slice(i * sg_cols + SG_STATE, (i + 1) * sg_cols)
        cs = slice(i * SG_STATE, (i + 1) * SG_STATE)
        if fillers:
            fillers.pop(0)()
        ar, ai = coef_ref[8, :, cs], coef_ref[9, :, cs]

        hr = jnp.zeros((SUBLANES, SG_STATE), F32)
        hi = jnp.zeros((SUBLANES, SG_STATE), F32)
        for j in range(nj):
            rows = slice(j * SUBLANES, (j + 1) * SUBLANES)
            tr, ti = _cmul(ar, ai, hr, hi)
            hr, hi = tr + bu_ref[rows, re], ti + bu_ref[rows, im]
            bu_ref[rows, re] = hr
            bu_ref[rows, im] = hi

        for t in range(3):
            k = 1 << t
            tr, ti = _cmul(coef_ref[2 * t, :, cs], coef_ref[2 * t + 1, :, cs],
                           pltpu.roll(hr, k, axis=0), pltpu.roll(hi, k, axis=0))
            hr, hi = hr + tr, hi + ti
        cr = jnp.broadcast_to(carry_ref[0:1, cs], hr.shape)
        cim = jnp.broadcast_to(carry_ref[1:2, cs], hi.shape)
        tr, ti = _cmul(coef_ref[6, :, cs], coef_ref[7, :, cs], cr, cim)
        hr, hi = hr + tr, hi + ti
        carry_ref[0:1, cs] = hr[SUBLANES - 1:SUBLANES, :]
        carry_ref[1:2, cs] = hi[SUBLANES - 1:SUBLANES, :]
        gr = jnp.where(rid == 0, cr, pltpu.roll(hr, 1, axis=0))
        gi = jnp.where(rid == 0, cim, pltpu.roll(hi, 1, axis=0))

        for j in range(0, nj, 2):
            outs = []
            for jj in (j, j + 1):
                rows = slice(jj * SUBLANES, (jj + 1) * SUBLANES)
                gr, gi = _cmul(ar, ai, gr, gi)
                outs.append((bu_ref[rows, re] + gr, bu_ref[rows, im] + gi))
            rows2 = slice(j * SUBLANES, (j + 2) * SUBLANES)
            hb_ref[rows2, re] = jnp.concatenate([outs[0][0], outs[1][0]], axis=0).astype(BF16)
            hb_ref[rows2, im] = jnp.concatenate([outs[0][1], outs[1][1]], axis=0).astype(BF16)

        y_parts.append(jnp.dot(hb_ref[:, i * sg_cols:(i + 1) * sg_cols], csg_ref[i],
                               preferred_element_type=F32))

    for proj in fillers:
        proj()
    y = jnp.concatenate(y_parts, axis=-1)
    u = uzp[:, :ssm_w].astype(F32)
    z = uzp[:, ssm_w:].astype(F32)
    y = jax.nn.gelu(y + d_ref[...] * u)
    gate = jnp.dot(y.astype(BF16), wglu_ref[...], preferred_element_type=F32) + bglu_ref[...]
    y = y * jax.nn.sigmoid(gate)
    y = y * (z * jax.nn.sigmoid(z))
    ysp = _rms(y, gs_ref[...]).astype(BF16)
    ys_ref[...] = jnp.dot(perm_ref[1], ysp, preferred_element_type=F32).astype(BF16)


def _attn_kernel(qt_ref, k_ref, aug_ref, vt_ref, o_ref, qc_ref, sa_ref, sb_ref, sc_ref, sd_ref,
                 ta_ref, tb_ref, tc_ref, td_ref, m_ref, acc_ref, *, tq, tk, td):
    qi = pl.program_id(2)
    n_h = LANES // HEAD_DIM

    zero = jnp.zeros((HEAD_DIM, tq), BF16)
    for h in range(n_h):
        for hh in range(n_h):
            qc_ref[h * HEAD_DIM:(h + 1) * HEAD_DIM, hh * tq:(hh + 1) * tq] = (
                qt_ref[0, h * HEAD_DIM:(h + 1) * HEAD_DIM, :] if h == hh else zero)
    r = lax.broadcasted_iota(jnp.int32, (LANES, n_h * tq), 0)
    c = lax.broadcasted_iota(jnp.int32, (LANES, n_h * tq), 1)
    ones = (r >= 0) & (r < 3) & (c < tq)
    for h in range(1, n_h):
        ones = ones | ((r >= 3 * h) & (r < 3 * h + 3) & (c >= h * tq) & (c < (h + 1) * tq))
    qc_ref[LANES:, :] = jnp.where(ones, 1.0, 0.0).astype(BF16)

    def pv(pb, start, h):
        size = pb.shape[0]
        lhs = jnp.concatenate([vt_ref[0, h * HEAD_DIM:(h + 1) * HEAD_DIM, pl.ds(start, size)],
                               jnp.ones((PV_ONES_ROWS, size), BF16)], axis=0)
        return jnp.dot(lhs, pb, preferred_element_type=F32)

    n_pairs = qi * (tq // (2 * tk))

    def key_rows(start, size=tk):
        return jnp.concatenate([k_ref[0, pl.ds(start, size), :],
                                aug_ref[0, 0, pl.ds(start, size), :]], axis=1)

    def scores(lhs, cols, s_ref, t_ref):
        s = jnp.dot(lhs, qc_ref[:, cols], preferred_element_type=F32)
        s_ref[:, cols] = s
        t_ref[:, cols] = jnp.max(s, axis=0, keepdims=True)

    def softmax_pv(start, cols, s_ref, t_ref):
        h = cols.start // tq
        qcols = slice(cols.start - h * tq, cols.stop - h * tq)
        m_old = m_ref[:, cols]
        m_new = jnp.maximum(m_old, t_ref[:, cols])
        alpha = jnp.exp2(m_old - m_new)
        pb = jnp.exp2(s_ref[:, cols] - m_new).astype(BF16)
        m_ref[:, cols] = m_new
        acc_ref[h, :, qcols] = alpha * acc_ref[h, :, qcols] + pv(pb, start, h)

    sets = (((sa_ref, ta_ref), (sb_ref, tb_ref)), ((sc_ref, tc_ref), (sd_ref, td_ref)))
    col_blocks = [slice(c, c + QCOL_BLOCK) for c in range(0, n_h * tq, QCOL_BLOCK)]

    def pair_body(p, cur, nxt):
        base = pl.multiple_of(p * (2 * tk), 2 * tk)
        for t in range(2):
            start = base + t * tk
            if nxt is not None:
                lhs = key_rows(start + 2 * tk)
            for cols in col_blocks:
                if nxt is not None:
                    scores(lhs, cols, *nxt[t])
                softmax_pv(start, cols, *cur[t])

    n_sub = tq // td
    d0 = pl.multiple_of(qi * tq, tq)
    kk = lax.broadcasted_iota(jnp.int32, (td, td), 0)
    qq = lax.broadcasted_iota(jnp.int32, (td, td), 1)
    causal = jnp.where(kk <= qq, 0.0, NEG_BIG)
    blocks = [(h, c) for c in range(n_sub) for h in range(n_h)]
    sq = {}
    for r in range(n_sub):
        lhs = key_rows(d0 + r * td, td)
        for h, c in blocks:
            if c >= r:
                cols = slice(h * tq + c * td, h * tq + (c + 1) * td)
                s = jnp.dot(lhs, qc_ref[:, cols], preferred_element_type=F32)
                sq[h, c, r] = s + causal if r == c else s
    first_pair = [(t, cols) for t in range(2) for cols in col_blocks]
    first_lhs = [key_rows(t * tk) for t in range(2)]
    state = {}
    n_done = 0
    for r in range(n_sub):
        for h, c in blocks:
            if c < r:
                continue
            s = sq[h, c, r]
            t = jnp.max(s, axis=0, keepdims=True)
            if r == 0:
                m_new = t
            else:
                m_old, acc = state[h, c]
                m_new = jnp.maximum(m_old, t)
                alpha = jnp.exp2(m_old - m_new)
            upd = pv(jnp.exp2(s - m_new).astype(BF16), d0 + r * td, h)
            acc = upd if r == 0 else alpha * acc + upd
            state[h, c] = (m_new, acc)
            if r == c:
                m_ref[:, h * tq + c * td:h * tq + (c + 1) * td] = m_new
                acc_ref[h, :, c * td:(c + 1) * td] = acc
            n_done += 1
            if n_done % 2 == 0 and first_pair:
                t0, cols0 = first_pair.pop(0)
                scores(first_lhs[t0], cols0, *sets[0][t0])
    for t0, cols0 in first_pair:
        scores(first_lhs[t0], cols0, *sets[0][t0])

    def pair(p, carry):
        for parity in range(2):
            @pl.when(p % 2 == parity)
            def _(parity=parity):
                pair_body(p, sets[parity], sets[1 - parity])
        return carry

    lax.fori_loop(0, n_pairs - 1, pair, 0)

    for parity in range(2):
        @pl.when((n_pairs > 0) & ((n_pairs - 1) % 2 == parity))
        def _(parity=parity):
            pair_body(n_pairs - 1, sets[parity], None)

    ot = jnp.concatenate(
        [acc_ref[h, :HEAD_DIM, :] * (1.0 / acc_ref[h, HEAD_DIM:HEAD_DIM + 1, :])
         for h in range(n_h)], axis=0)
    o_ref[0] = ot.T.astype(BF16)


def _out_proj_kernel(ys_ref, oa_ref, za_ref, ga_ref, w_ref, x_ref, gp_ref, out_ref, *, ssm_w):
    z = za_ref[...].astype(F32)
    ya = _rms(oa_ref[...].astype(F32) * (z * jax.nn.sigmoid(z)), ga_ref[...]).astype(BF16)
    y = (jnp.dot(ys_ref[...], w_ref[:ssm_w, :], preferred_element_type=F32)
         + jnp.dot(ya, w_ref[ssm_w:, :], preferred_element_type=F32))
    out_ref[...] = x_ref[...] + _rms(y, gp_ref[...])


def _s5_params(a_re, a_im, log_dt, b_re, b_im, c_re, c_im, nj):
    g, n = a_re.shape
    h = b_re.shape[-1]
    n_sg = g // SG_GROUPS
    dt = jnp.exp(log_dt)[:, None]
    mag, ang = jnp.exp(dt * a_re), dt * a_im
    abar_re, abar_im = mag * jnp.cos(ang), mag * jnp.sin(ang)
    den = a_re * a_re + a_im * a_im
    nr, ni = abar_re - 1.0, abar_im
    coef_re = (nr * a_re + ni * a_im) / den
    coef_im = (ni * a_re - nr * a_im) / den
    bbar_re = coef_re[..., None] * b_re - coef_im[..., None] * b_im
    bbar_im = coef_re[..., None] * b_im + coef_im[..., None] * b_re
    eye = jnp.eye(SG_GROUPS, dtype=F32)
    bb = jnp.stack([bbar_re, bbar_im]).reshape(2, n_sg, SG_GROUPS, n, h)
    bsg = jnp.einsum('pignh,gk->ighpkn', bb, eye).reshape(n_sg, SG_GROUPS * h, 2 * SG_GROUPS * n)
    cc = jnp.stack([c_re, -c_im]).reshape(2, n_sg, SG_GROUPS, h, n)
    csg = jnp.einsum('pighn,gk->ipgnkh', cc, eye).reshape(n_sg, 2 * SG_GROUPS * n, SG_GROUPS * h)

    def apow(m):
        m = jnp.asarray(m, F32)[:, None, None]
        pm, pa = jnp.exp(m * (dt * a_re)), m * ang
        return (pm * jnp.cos(pa)).reshape(-1, g * n), (pm * jnp.sin(pa)).reshape(-1, g * n)

    rows = jnp.arange(SUBLANES)
    coef = []
    for k in (1, 2, 4):
        pr, pi = apow([nj * k])
        live = (rows >= k)[:, None]
        coef += [jnp.where(live, pr, 0.0), jnp.where(live, pi, 0.0)]
    pr, pi = apow(nj * (rows + 1))
    coef += [pr, pi]
    pr, pi = apow(jnp.ones((SUBLANES,)))
    coef += [pr, pi]
    return bsg.astype(BF16), csg.astype(BF16), jnp.stack(coef)


def _layer(x, norm_pre_g, w_in, b_forget, a_re, a_im, log_dt, b_re, b_im, c_re, c_im, d_skip,
           w_glu, b_glu, g_ssm, g_attn, w_out, norm_post_g):
    bsz, seq, d_model = x.shape
    n_heads = b_forget.shape[0]
    attn_w = n_heads * HEAD_DIM
    ssm_w = d_skip.shape[0]
    tokens = bsz * seq
    tm, to, tq = TM_PROJ, TM_OUT, TQ_ATTN
    assert seq % tm == 0 and tokens % to == 0 and seq % tq == 0
    assert ssm_w % LANES == 0 and attn_w % LANES == 0 and n_heads <= 2 * SUBLANES
    nt = seq // tm
    x2 = x.reshape(tokens, d_model)
    row = lambda v: v.reshape(1, -1).astype(F32)

    c0 = 2 * ssm_w
    w_q, w_k, w_v, w_za, w_f = (w_in[:, c0 + j * attn_w:c0 + (j + 1) * attn_w] for j in range(5))
    w_main = jnp.concatenate([w_in[:, :c0], w_k, w_za], axis=1).astype(BF16)
    w_t = jnp.concatenate([w_q.T, w_v.T, w_f.T,
                           jnp.zeros((2 * SUBLANES - n_heads, d_model), F32)], axis=0).astype(BF16)

    assert tm % (2 * SUBLANES * SUBLANES) == 0 and (tm // SUBLANES) & (tm // SUBLANES - 1) == 0
    bsg, csg, coef = _s5_params(a_re, a_im, log_dt, b_re, b_im, c_re, c_im, tm // SUBLANES)
    n_state = a_re.size
    src_row = (jnp.arange(tm) % SUBLANES) * (tm // SUBLANES) + jnp.arange(tm) // SUBLANES
    perm = (src_row[:, None] == jnp.arange(tm)[None, :])
    perms = jnp.stack([perm, perm.T]).astype(BF16)
    const2 = lambda i: (0, 0)
    const3 = lambda i: (0, 0, 0)
    time_major = lambda i: (i // nt, 0, i % nt)

    ys, k, za, qt, vt, ft = pl.pallas_call(
        functools.partial(_in_proj_s5_kernel, ssm_w=ssm_w, attn_w=attn_w, n_heads=n_heads, tl=tm,
                          tiles_per_seq=nt),
        grid=(tokens // tm,),
        in_specs=[pl.BlockSpec((tm, d_model), lambda i: (i, 0)),
                  pl.BlockSpec((1, d_model), const2),
                  pl.BlockSpec(w_main.shape, const2),
                  pl.BlockSpec(w_t.shape, const2),
                  pl.BlockSpec(perms.shape, const3),
                  pl.BlockSpec(bsg.shape, const3),
                  pl.BlockSpec(csg.shape, const3),
                  pl.BlockSpec(coef.shape, const3),
                  pl.BlockSpec((1, ssm_w), const2),
                  pl.BlockSpec((ssm_w, ssm_w), const2),
                  pl.BlockSpec((1, ssm_w), const2),
                  pl.BlockSpec((1, ssm_w), const2)],
        out_specs=[pl.BlockSpec((tm, ssm_w), lambda i: (i, 0)),
                   pl.BlockSpec((tm, attn_w), lambda i: (i, 0)),
                   pl.BlockSpec((tm, attn_w), lambda i: (i, 0)),
                   pl.BlockSpec((1, attn_w, tm), time_major),
                   pl.BlockSpec((1, attn_w, tm), time_major),
                   pl.BlockSpec((1, n_heads, tm), time_major)],
        out_shape=[jax.ShapeDtypeStruct((tokens, ssm_w), BF16),
                   jax.ShapeDtypeStruct((tokens, attn_w), BF16),
                   jax.ShapeDtypeStruct((tokens, attn_w), BF16),
                   jax.ShapeDtypeStruct((bsz, attn_w, seq), BF16),
                   jax.ShapeDtypeStruct((bsz, attn_w, seq), BF16),
                   jax.ShapeDtypeStruct((bsz, n_heads, seq), F32)],
        scratch_shapes=[pltpu.VMEM((tm, 2 * n_state), F32),
                        pltpu.VMEM((tm, 2 * n_state), BF16),
                        pltpu.VMEM((2, n_state), F32)],
        compiler_params=_cparams("arbitrary"),
        name="in_proj_s5",
    )(x2, row(norm_pre_g), w_main, w_t, perms, bsg, csg, coef, row(d_skip), w_glu.astype(BF16),
      row(b_glu), row(g_ssm))

    n_hp = n_heads // (LANES // HEAD_DIM)
    aug = pl.pallas_call(
        functools.partial(_forget_cum_kernel, seq=seq, n_hp=n_hp, chunk=CUM_CHUNK),
        grid=(bsz,),
        in_specs=[pl.BlockSpec((1, n_heads, seq), lambda b: (b, 0, 0)),
                  pl.BlockSpec((n_heads, 1), lambda b: (0, 0))],
        out_specs=pl.BlockSpec((1, n_hp, seq, LANES), lambda b: (b, 0, 0, 0)),
        out_shape=jax.ShapeDtypeStruct((bsz, n_hp, seq, LANES), BF16),
        compiler_params=_cparams("arbitrary"),
        name="forget_cum",
    )(ft, b_forget.reshape(n_heads, 1).astype(F32))

    tk, td = TK_ATTN, TD_ATTN
    assert tq % (2 * tk) == 0 and tq % td == 0 and 3 * (LANES // HEAD_DIM) <= LANES
    nq = seq // tq
    wq = (LANES // HEAD_DIM) * tq
    oa = pl.pallas_call(
        functools.partial(_attn_kernel, tq=tq, tk=tk, td=td),
        grid=(bsz, n_hp, nq),
        in_specs=[pl.BlockSpec((1, LANES, tq), lambda b, p, i: (b, p, i)),
                  pl.BlockSpec((1, seq, LANES), lambda b, p, i: (b, 0, p)),
                  pl.BlockSpec((1, 1, seq, LANES), lambda b, p, i: (b, p, 0, 0)),
                  pl.BlockSpec((1, LANES, seq), lambda b, p, i: (b, p, 0))],
        out_specs=pl.BlockSpec((1, tq, LANES), lambda b, p, i: (b, i, p)),
        out_shape=jax.ShapeDtypeStruct((bsz, seq, attn_w), BF16),
        scratch_shapes=[pltpu.VMEM((2 * LANES, wq), BF16),
                        pltpu.VMEM((tk, wq + LANES), F32),
                        pltpu.VMEM((tk, wq + LANES), F32),
                        pltpu.VMEM((tk, wq + LANES), F32),
                        pltpu.VMEM((tk, wq + LANES), F32),
                        pltpu.VMEM((1, wq), F32),
                        pltpu.VMEM((1, wq), F32),
                        pltpu.VMEM((1, wq), F32),
                        pltpu.VMEM((1, wq), F32),
                        pltpu.VMEM((1, wq), F32),
                        pltpu.VMEM((LANES // HEAD_DIM, HEAD_DIM + PV_ONES_ROWS, tq), F32)],
        compiler_params=_cparams("arbitrary", "arbitrary", "arbitrary"),
        name="fox_attn",
    )(qt, k.reshape(bsz, seq, attn_w), aug, vt)

    out = pl.pallas_call(
        functools.partial(_out_proj_kernel, ssm_w=ssm_w),
        grid=(tokens // to,),
        in_specs=[pl.BlockSpec((to, ssm_w), lambda i: (i, 0)),
                  pl.BlockSpec((to, attn_w), lambda i: (i, 0)),
                  pl.BlockSpec((to, attn_w), lambda i: (i, 0)),
                  pl.BlockSpec((1, attn_w), lambda i: (0, 0)),
                  pl.BlockSpec((ssm_w + attn_w, d_model), lambda i: (0, 0)),
                  pl.BlockSpec((to, d_model), lambda i: (i, 0)),
                  pl.BlockSpec((1, d_model), lambda i: (0, 0))],
        out_specs=pl.BlockSpec((to, d_model), lambda i: (i, 0)),
        out_shape=jax.ShapeDtypeStruct((tokens, d_model), F32),
        compiler_params=_cparams("arbitrary"),
        name="out_proj",
    )(ys, oa.reshape(tokens, attn_w), za, row(g_attn), w_out.astype(BF16), x2, row(norm_post_g))
    return out.reshape(bsz, seq, d_model)


def kernel(x, norm_pre_g, w_in, b_forget, ssm_a_re, ssm_a_im, ssm_log_dt, ssm_b_re, ssm_b_im,
           ssm_c_re, ssm_c_im, ssm_d, w_glu, b_glu, g_ssm, g_attn, w_out, norm_post_g):
    h = x
    for layer in range(norm_pre_g.shape[0]):
        h = _layer(h, norm_pre_g[layer], w_in[layer], b_forget[layer], ssm_a_re[layer],
                   ssm_a_im[layer], ssm_log_dt[layer], ssm_b_re[layer], ssm_b_im[layer],
                   ssm_c_re[layer], ssm_c_im[layer], ssm_d[layer], w_glu[layer], b_glu[layer],
                   g_ssm[layer], g_attn[layer], w_out[layer], norm_post_g[layer])
    return h
```

```python
import functools
import math

import jax
import jax.numpy as jnp
from jax import lax
from jax.experimental import pallas as pl
from jax.experimental.pallas import tpu as pltpu

F32 = jnp.float32
BF16 = jnp.bfloat16

SSM_GROUP_CH = 16
SSM_STATE = 64
HEAD_DIM = 64
RMS_EPS = 1e-6

LANES = 128
SUBLANES = 8
VMEM_LIMIT_BYTES = 56 * 1024 * 1024

TM_PROJ = 512
TM_OUT = 1024
TQ_ATTN = 1024
TK_ATTN = 512
TD_ATTN = 256
QCOL_BLOCK = 256
CUM_CHUNK = 128
PV_ONES_ROWS = 2 * SUBLANES
LOG2E = math.log2(math.e)
NT_DIMS = (((1,), (1,)), ((), ()))
SG_GROUPS = LANES // SSM_GROUP_CH
SG_STATE = SG_GROUPS * SSM_STATE
NEG_BIG = -1e30


def _cparams(*sem):
    return pltpu.CompilerParams(dimension_semantics=sem, vmem_limit_bytes=VMEM_LIMIT_BYTES)


def _rms(v, g):
    return v * lax.rsqrt(jnp.mean(v * v, axis=-1, keepdims=True) + RMS_EPS) * g


def _forget_cum_kernel(ft_ref, bf_ref, aug_ref, *, seq, n_hp, chunk):
    f = ft_ref[0] + bf_ref[...]
    lf = jnp.minimum(f, 0.0) - jnp.log1p(jnp.exp(-jnp.abs(f)))
    lane = lax.broadcasted_iota(jnp.int32, lf.shape, 1)
    k = 1
    while k < seq:
        lf = lf + jnp.where(lane >= k, pltpu.roll(lf, k, axis=1), 0.0)
        k *= 2
    nb = lf * (-LOG2E)
    hi = nb.astype(BF16).astype(F32)
    mid = (nb - hi).astype(BF16).astype(F32)
    lo = (nb - hi - mid).astype(BF16).astype(F32)
    parts = (hi, mid, lo)
    rid = lax.broadcasted_iota(jnp.int32, (SUBLANES, seq), 0)
    eye = (lax.broadcasted_iota(jnp.int32, (chunk, chunk), 0)
           == lax.broadcasted_iota(jnp.int32, (chunk, chunk), 1)).astype(BF16)
    pad = jnp.zeros((LANES - SUBLANES, seq), F32)
    for p in range(n_hp):
        rows = jnp.zeros((SUBLANES, seq), F32)
        for j in range(LANES // HEAD_DIM):
            for t, part in enumerate(parts):
                src = part[2 * p + j:2 * p + j + 1, :]
                rows = jnp.where(rid == 3 * j + t, jnp.broadcast_to(src, rows.shape), rows)
        bmat = jnp.concatenate([rows, pad], axis=0).astype(BF16)
        for c in range(seq // chunk):
            blk = lax.dot_general(eye, bmat[:, c * chunk:(c + 1) * chunk], NT_DIMS,
                                  preferred_element_type=F32)
            aug_ref[0, p, c * chunk:(c + 1) * chunk, :] = blk.astype(BF16)


def _cmul(ar, ai, xr, xi):
    return ar * xr - ai * xi, ar * xi + ai * xr


def _in_proj_s5_kernel(x_ref, g_ref, w_ref, wt_ref, perm_ref, bsg_ref, csg_ref, coef_ref, d_ref,
                       wglu_ref, bglu_ref, gs_ref, ys_ref, k_ref, za_ref, qt_ref, vt_ref, ft_ref,
                       bu_ref, hb_ref, carry_ref, *, ssm_w, attn_w, n_heads, tl, tiles_per_seq):
    n_sg = ssm_w // LANES
    sg_cols = 2 * SG_STATE
    nj = tl // SUBLANES
    c0 = 2 * ssm_w

    @pl.when(pl.program_id(0) % tiles_per_seq == 0)
    def _():
        carry_ref[...] = jnp.zeros_like(carry_ref)

    xb = _rms(x_ref[...], g_ref[...]).astype(BF16)
    uz = jnp.dot(xb, w_ref[:, :c0], preferred_element_type=F32).astype(BF16)
    uzp = jnp.dot(perm_ref[0], uz, preferred_element_type=F32).astype(BF16)

    def proj_k():
        k_ref[...] = jnp.dot(xb, w_ref[:, c0:c0 + attn_w],
                             preferred_element_type=F32).astype(BF16)

    def proj_za():
        za_ref[...] = jnp.dot(xb, w_ref[:, c0 + attn_w:], preferred_element_type=F32).astype(BF16)

    def proj_qt():
        qt = lax.dot_general(wt_ref[:attn_w], xb, NT_DIMS, preferred_element_type=F32)
        qt_ref[0] = (qt * (HEAD_DIM ** -0.5 * LOG2E)).astype(BF16)

    def proj_vt():
        vt_ref[0] = lax.dot_general(wt_ref[attn_w:2 * attn_w], xb, NT_DIMS,
                                    preferred_element_type=F32).astype(BF16)
        ft = lax.dot_general(wt_ref[2 * attn_w:], xb, NT_DIMS, preferred_element_type=F32)
        ft_ref[0] = ft[:n_heads]

    fillers = [proj_k, proj_za, proj_qt, proj_vt]

    for i in range(n_sg):
        bu_ref[:, i * sg_cols:(i + 1) * sg_cols] = jnp.dot(
            uzp[:, i * LANES:(i + 1) * LANES], bsg_ref[i], preferred_element_type=F32)

    rid = lax.broadcasted_iota(jnp.int32, (SUBLANES, SG_STATE), 0)
    y_parts = []
    for i in range(n_sg):
        re = slice(i * sg_cols, i * sg_cols + SG_STATE)
        im = slice(i * sg_cols + SG_STATE, (i + 1) * sg_cols)
        cs = slice(i * SG_STATE, (i + 1) * SG_STATE)
        if fillers:
            fillers.pop(0)()
        ar, ai = coef_ref[8, :, cs], coef_ref[9, :, cs]

        hr = jnp.zeros((SUBLANES, SG_STATE), F32)
        hi = jnp.zeros((SUBLANES, SG_STATE), F32)
        for j in range(nj):
            rows = slice(j * SUBLANES, (j + 1) * SUBLANES)
            tr, ti = _cmul(ar, ai, hr, hi)
            hr, hi = tr + bu_ref[rows, re], ti + bu_ref[rows, im]
            bu_ref[rows, re] = hr
            bu_ref[rows, im] = hi

        for t in range(3):
            k = 1 << t
            tr, ti = _cmul(coef_ref[2 * t, :, cs], coef_ref[2 * t + 1, :, cs],
                           pltpu.roll(hr, k, axis=0), pltpu.roll(hi, k, axis=0))
            hr, hi = hr + tr, hi + ti
        cr = jnp.broadcast_to(carry_ref[0:1, cs], hr.shape)
        cim = jnp.broadcast_to(carry_ref[1:2, cs], hi.shape)
        tr, ti = _cmul(coef_ref[6, :, cs], coef_ref[7, :, cs], cr, cim)
        hr, hi = hr + tr, hi + ti
        carry_ref[0:1, cs] = hr[SUBLANES - 1:SUBLANES, :]
        carry_ref[1:2, cs] = hi[SUBLANES - 1:SUBLANES, :]
        gr = jnp.where(rid == 0, cr, pltpu.roll(hr, 1, axis=0))
        gi = jnp.where(rid == 0, cim, pltpu.roll(hi, 1, axis=0))

        for j in range(0, nj, 2):
            outs = []
            for jj in (j, j + 1):
                rows = slice(jj * SUBLANES, (jj + 1) * SUBLANES)
                gr, gi = _cmul(ar, ai, gr, gi)
                outs.append((bu_ref[rows, re] + gr, bu_ref[rows, im] + gi))
            rows2 = slice(j * SUBLANES, (j + 2) * SUBLANES)
            hb_ref[rows2, re] = jnp.concatenate([outs[0][0], outs[1][0]], axis=0).astype(BF16)
            hb_ref[rows2, im] = jnp.concatenate([outs[0][1], outs[1][1]], axis=0).astype(BF16)

        y_parts.append(jnp.dot(hb_ref[:, i * sg_cols:(i + 1) * sg_cols], csg_ref[i],
                               preferred_element_type=F32))

    for proj in fillers:
        proj()
    y = jnp.concatenate(y_parts, axis=-1)
    u = uzp[:, :ssm_w].astype(F32)
    z = uzp[:, ssm_w:].astype(F32)
    y = jax.nn.gelu(y + d_ref[...] * u)
    gate = jnp.dot(y.astype(BF16), wglu_ref[...], preferred_element_type=F32) + bglu_ref[...]
    y = y * jax.nn.sigmoid(gate)
    y = y * (z * jax.nn.sigmoid(z))
    ysp = _rms(y, gs_ref[...]).astype(BF16)
    ys_ref[...] = jnp.dot(perm_ref[1], ysp, preferred_element_type=F32).astype(BF16)


def _attn_kernel(qt_ref, k_ref, aug_ref, vt_ref, o_ref, qc_ref, sa_ref, sb_ref, sc_ref, sd_ref,
                 ta_ref, tb_ref, tc_ref, td_ref, m_ref, acc_ref, *, tq, tk, td):
    qi = pl.program_id(2)
    n_h = LANES // HEAD_DIM

    zero = jnp.zeros((HEAD_DIM, tq), BF16)
    for h in range(n_h):
        for hh in range(n_h):
            qc_ref[h * HEAD_DIM:(h + 1) * HEAD_DIM, hh * tq:(hh + 1) * tq] = (
                qt_ref[0, h * HEAD_DIM:(h + 1) * HEAD_DIM, :] if h == hh else zero)
    r = lax.broadcasted_iota(jnp.int32, (LANES, n_h * tq), 0)
    c = lax.broadcasted_iota(jnp.int32, (LANES, n_h * tq), 1)
    ones = (r >= 0) & (r < 3) & (c < tq)
    for h in range(1, n_h):
        ones = ones | ((r >= 3 * h) & (r < 3 * h + 3) & (c >= h * tq) & (c < (h + 1) * tq))
    qc_ref[LANES:, :] = jnp.where(ones, 1.0, 0.0).astype(BF16)

    def pv(pb, start, h):
        size = pb.shape[0]
        lhs = jnp.concatenate([vt_ref[0, h * HEAD_DIM:(h + 1) * HEAD_DIM, pl.ds(start, size)],
                               jnp.ones((PV_ONES_ROWS, size), BF16)], axis=0)
        return jnp.dot(lhs, pb, preferred_element_type=F32)

    n_pairs = qi * (tq // (2 * tk))

    def key_rows(start, size=tk):
        return jnp.concatenate([k_ref[0, pl.ds(start, size), :],
                                aug_ref[0, 0, pl.ds(start, size), :]], axis=1)

    def scores(lhs, cols, s_ref, t_ref):
        s = jnp.dot(lhs, qc_ref[:, cols], preferred_element_type=F32)
        s_ref[:, cols] = s
        t_ref[:, cols] = jnp.max(s, axis=0, keepdims=True)

    def softmax_pv(start, cols, s_ref, t_ref):
        h = cols.start // tq
        qcols = slice(cols.start - h * tq, cols.stop - h * tq)
        m_old = m_ref[:, cols]
        m_new = jnp.maximum(m_old, t_ref[:, cols])
        alpha = jnp.exp2(m_old - m_new)
        pb = jnp.exp2(s_ref[:, cols] - m_new).astype(BF16)
        m_ref[:, cols] = m_new
        acc_ref[h, :, qcols] = alpha * acc_ref[h, :, qcols] + pv(pb, start, h)

    sets = (((sa_ref, ta_ref), (sb_ref, tb_ref)), ((sc_ref, tc_ref), (sd_ref, td_ref)))
    col_blocks = [slice(c, c + QCOL_BLOCK) for c in range(0, n_h * tq, QCOL_BLOCK)]

    def pair_body(p, cur, nxt):
        base = pl.multiple_of(p * (2 * tk), 2 * tk)
        for t in range(2):
            start = base + t * tk
            if nxt is not None:
                lhs = key_rows(start + 2 * tk)
            for cols in col_blocks:
                if nxt is not None:
                    scores(lhs, cols, *nxt[t])
                softmax_pv(start, cols, *cur[t])

    n_sub = tq // td
    d0 = pl.multiple_of(qi * tq, tq)
    kk = lax.broadcasted_iota(jnp.int32, (td, td), 0)
    qq = lax.broadcasted_iota(jnp.int32, (td, td), 1)
    causal = jnp.where(kk <= qq, 0.0, NEG_BIG)
    blocks = [(h, c) for c in range(n_sub) for h in range(n_h)]
    sq = {}
    for r in range(n_sub):
        lhs = key_rows(d0 + r * td, td)
        for h, c in blocks:
            if c >= r:
                cols = slice(h * tq + c * td, h * tq + (c + 1) * td)
                s = jnp.dot(lhs, qc_ref[:, cols], preferred_element_type=F32)
                sq[h, c, r] = s + causal if r == c else s
    first_pair = [(t, cols) for t in range(2) for cols in col_blocks]
    first_lhs = [key_rows(t * tk) for t in range(2)]
    state = {}
    n_done = 0
    for r in range(n_sub):
        for h, c in blocks:
            if c < r:
                continue
            s = sq[h, c, r]
            t = jnp.max(s, axis=0, keepdims=True)
            if r == 0:
                m_new = t
            else:
                m_old, acc = state[h, c]
                m_new = jnp.maximum(m_old, t)
                alpha = jnp.exp2(m_old - m_new)
            upd = pv(jnp.exp2(s - m_new).astype(BF16), d0 + r * td, h)
            acc = upd if r == 0 else alpha * acc + upd
            state[h, c] = (m_new, acc)
            if r == c:
                m_ref[:, h * tq + c * td:h * tq + (c + 1) * td] = m_new
                acc_ref[h, :, c * td:(c + 1) * td] = acc
            n_done += 1
            if n_done % 2 == 0 and first_pair:
                t0, cols0 = first_pair.pop(0)
                scores(first_lhs[t0], cols0, *sets[0][t0])
    for t0, cols0 in first_pair:
        scores(first_lhs[t0], cols0, *sets[0][t0])

    def pair(p, carry):
        for parity in range(2):
            @pl.when(p % 2 == parity)
            def _(parity=parity):
                pair_body(p, sets[parity], sets[1 - parity])
        return carry

    lax.fori_loop(0, n_pairs - 1, pair, 0)

    for parity in range(2):
        @pl.when((n_pairs > 0) & ((n_pairs - 1) % 2 == parity))
        def _(parity=parity):
            pair_body(n_pairs - 1, sets[parity], None)

    ot = jnp.concatenate(
        [acc_ref[h, :HEAD_DIM, :] * (1.0 / acc_ref[h, HEAD_DIM:HEAD_DIM + 1, :])
         for h in range(n_h)], axis=0)
    o_ref[0] = ot.T.astype(BF16)


def _out_proj_kernel(ys_ref, oa_ref, za_ref, ga_ref, w_ref, x_ref, gp_ref, out_ref, *, ssm_w):
    z = za_ref[...].astype(F32)
    ya = _rms(oa_ref[...].astype(F32) * (z * jax.nn.sigmoid(z)), ga_ref[...]).astype(BF16)
    y = (jnp.dot(ys_ref[...], w_ref[:ssm_w, :], preferred_element_type=F32)
         + jnp.dot(ya, w_ref[ssm_w:, :], preferred_element_type=F32))
    out_ref[...] = x_ref[...] + _rms(y, gp_ref[...])


def _s5_params(a_re, a_im, log_dt, b_re, b_im, c_re, c_im, nj):
    g, n = a_re.shape
    h = b_re.shape[-1]
    n_sg = g // SG_GROUPS
    dt = jnp.exp(log_dt)[:, None]
    mag, ang = jnp.exp(dt * a_re), dt * a_im
    abar_re, abar_im = mag * jnp.cos(ang), mag * jnp.sin(ang)
    den = a_re * a_re + a_im * a_im
    nr, ni = abar_re - 1.0, abar_im
    coef_re = (nr * a_re + ni * a_im) / den
    coef_im = (ni * a_re - nr * a_im) / den
    bbar_re = coef_re[..., None] * b_re - coef_im[..., None] * b_im
    bbar_im = coef_re[..., None] * b_im + coef_im[..., None] * b_re
    eye = jnp.eye(SG_GROUPS, dtype=F32)
    bb = jnp.stack([bbar_re, bbar_im]).reshape(2, n_sg, SG_GROUPS, n, h)
    bsg = jnp.einsum('pignh,gk->ighpkn', bb, eye).reshape(n_sg, SG_GROUPS * h, 2 * SG_GROUPS * n)
    cc = jnp.stack([c_re, -c_im]).reshape(2, n_sg, SG_GROUPS, h, n)
    csg = jnp.einsum('pighn,gk->ipgnkh', cc, eye).reshape(n_sg, 2 * SG_GROUPS * n, SG_GROUPS * h)

    def apow(m):
        m = jnp.asarray(m, F32)[:, None, None]
        pm, pa = jnp.exp(m * (dt * a_re)), m * ang
        return (pm * jnp.cos(pa)).reshape(-1, g * n), (pm * jnp.sin(pa)).reshape(-1, g * n)

    rows = jnp.arange(SUBLANES)
    coef = []
    for k in (1, 2, 4):
        pr, pi = apow([nj * k])
        live = (rows >= k)[:, None]
        coef += [jnp.where(live, pr, 0.0), jnp.where(live, pi, 0.0)]
    pr, pi = apow(nj * (rows + 1))
    coef += [pr, pi]
    pr, pi = apow(jnp.ones((SUBLANES,)))
    coef += [pr, pi]
    return bsg.astype(BF16), csg.astype(BF16), jnp.stack(coef)


def _layer(x, norm_pre_g, w_in, b_forget, a_re, a_im, log_dt, b_re, b_im, c_re, c_im, d_skip,
           w_glu, b_glu, g_ssm, g_attn, w_out, norm_post_g):
    bsz, seq, d_model = x.shape
    n_heads = b_forget.shape[0]
    attn_w = n_heads * HEAD_DIM
    ssm_w = d_skip.shape[0]
    tokens = bsz * seq
    tm, to, tq = TM_PROJ, TM_OUT, TQ_ATTN
    assert seq % tm == 0 and tokens % to == 0 and seq % tq == 0
    assert ssm_w % LANES == 0 and attn_w % LANES == 0 and n_heads <= 2 * SUBLANES
    nt = seq // tm
    x2 = x.reshape(tokens, d_model)
    row = lambda v: v.reshape(1, -1).astype(F32)

    c0 = 2 * ssm_w
    w_q, w_k, w_v, w_za, w_f = (w_in[:, c0 + j * attn_w:c0 + (j + 1) * attn_w] for j in range(5))
    w_main = jnp.concatenate([w_in[:, :c0], w_k, w_za], axis=1).astype(BF16)
    w_t = jnp.concatenate([w_q.T, w_v.T, w_f.T,
                           jnp.zeros((2 * SUBLANES - n_heads, d_model), F32)], axis=0).astype(BF16)

    assert tm % (2 * SUBLANES * SUBLANES) == 0 and (tm // SUBLANES) & (tm // SUBLANES - 1) == 0
    bsg, csg, coef = _s5_params(a_re, a_im, log_dt, b_re, b_im, c_re, c_im, tm // SUBLANES)
    n_state = a_re.size
    src_row = (jnp.arange(tm) % SUBLANES) * (tm // SUBLANES) + jnp.arange(tm) // SUBLANES
    perm = (src_row[:, None] == jnp.arange(tm)[None, :])
    perms = jnp.stack([perm, perm.T]).astype(BF16)
    const2 = lambda i: (0, 0)
    const3 = lambda i: (0, 0, 0)
    time_major = lambda i: (i // nt, 0, i % nt)

    ys, k, za, qt, vt, ft = pl.pallas_call(
        functools.partial(_in_proj_s5_kernel, ssm_w=ssm_w, attn_w=attn_w, n_heads=n_heads, tl=tm,
                          tiles_per_seq=nt),
        grid=(tokens // tm,),
        in_specs=[pl.BlockSpec((tm, d_model), lambda i: (i, 0)),
                  pl.BlockSpec((1, d_model), const2),
                  pl.BlockSpec(w_main.shape, const2),
                  pl.BlockSpec(w_t.shape, const2),
                  pl.BlockSpec(perms.shape, const3),
                  pl.BlockSpec(bsg.shape, const3),
                  pl.BlockSpec(csg.shape, const3),
                  pl.BlockSpec(coef.shape, const3),
                  pl.BlockSpec((1, ssm_w), const2),
                  pl.BlockSpec((ssm_w, ssm_w), const2),
                  pl.BlockSpec((1, ssm_w), const2),
                  pl.BlockSpec((1, ssm_w), const2)],
        out_specs=[pl.BlockSpec((tm, ssm_w), lambda i: (i, 0)),
                   pl.BlockSpec((tm, attn_w), lambda i: (i, 0)),
                   pl.BlockSpec((tm, attn_w), lambda i: (i, 0)),
                   pl.BlockSpec((1, attn_w, tm), time_major),
                   pl.BlockSpec((1, attn_w, tm), time_major),
                   pl.BlockSpec((1, n_heads, tm), time_major)],
        out_shape=[jax.ShapeDtypeStruct((tokens, ssm_w), BF16),
                   jax.ShapeDtypeStruct((tokens, attn_w), BF16),
                   jax.ShapeDtypeStruct((tokens, attn_w), BF16),
                   jax.ShapeDtypeStruct((bsz, attn_w, seq), BF16),
                   jax.ShapeDtypeStruct((bsz, attn_w, seq), BF16),
                   jax.ShapeDtypeStruct((bsz, n_heads, seq), F32)],
        scratch_shapes=[pltpu.VMEM((tm, 2 * n_state), F32),
                        pltpu.VMEM((tm, 2 * n_state), BF16),
                        pltpu.VMEM((2, n_state), F32)],
        compiler_params=_cparams("arbitrary"),
        name="in_proj_s5",
    )(x2, row(norm_pre_g), w_main, w_t, perms, bsg, csg, coef, row(d_skip), w_glu.astype(BF16),
      row(b_glu), row(g_ssm))

    n_hp = n_heads // (LANES // HEAD_DIM)
    aug = pl.pallas_call(
        functools.partial(_forget_cum_kernel, seq=seq, n_hp=n_hp, chunk=CUM_CHUNK),
        grid=(bsz,),
        in_specs=[pl.BlockSpec((1, n_heads, seq), lambda b: (b, 0, 0)),
                  pl.BlockSpec((n_heads, 1), lambda b: (0, 0))],
        out_specs=pl.BlockSpec((1, n_hp, seq, LANES), lambda b: (b, 0, 0, 0)),
        out_shape=jax.ShapeDtypeStruct((bsz, n_hp, seq, LANES), BF16),
        compiler_params=_cparams("arbitrary"),
        name="forget_cum",
    )(ft, b_forget.reshape(n_heads, 1).astype(F32))

    tk, td = TK_ATTN, TD_ATTN
    assert tq % (2 * tk) == 0 and tq % td == 0 and 3 * (LANES // HEAD_DIM) <= LANES
    nq = seq // tq
    wq = (LANES // HEAD_DIM) * tq
    oa = pl.pallas_call(
        functools.partial(_attn_kernel, tq=tq, tk=tk, td=td),
        grid=(bsz, n_hp, nq),
        in_specs=[pl.BlockSpec((1, LANES, tq), lambda b, p, i: (b, p, i)),
                  pl.BlockSpec((1, seq, LANES), lambda b, p, i: (b, 0, p)),
                  pl.BlockSpec((1, 1, seq, LANES), lambda b, p, i: (b, p, 0, 0)),
                  pl.BlockSpec((1, LANES, seq), lambda b, p, i: (b, p, 0))],
        out_specs=pl.BlockSpec((1, tq, LANES), lambda b, p, i: (b, i, p)),
        out_shape=jax.ShapeDtypeStruct((bsz, seq, attn_w), BF16),
        scratch_shapes=[pltpu.VMEM((2 * LANES, wq), BF16),
                        pltpu.VMEM((tk, wq + LANES), F32),
                        pltpu.VMEM((tk, wq + LANES), F32),
                        pltpu.VMEM((tk, wq + LANES), F32),
                        pltpu.VMEM((tk, wq + LANES), F32),
                        pltpu.VMEM((1, wq), F32),
                        pltpu.VMEM((1, wq), F32),
                        pltpu.VMEM((1, wq), F32),
                        pltpu.VMEM((1, wq), F32),
                        pltpu.VMEM((1, wq), F32),
                        pltpu.VMEM((LANES // HEAD_DIM, HEAD_DIM + PV_ONES_ROWS, tq), F32)],
        compiler_params=_cparams("arbitrary", "arbitrary", "arbitrary"),
        name="fox_attn",
    )(qt, k.reshape(bsz, seq, attn_w), aug, vt)

    out = pl.pallas_call(
        functools.partial(_out_proj_kernel, ssm_w=ssm_w),
        grid=(tokens // to,),
        in_specs=[pl.BlockSpec((to, ssm_w), lambda i: (i, 0)),
                  pl.BlockSpec((to, attn_w), lambda i: (i, 0)),
                  pl.BlockSpec((to, attn_w), lambda i: (i, 0)),
                  pl.BlockSpec((1, attn_w), lambda i: (0, 0)),
                  pl.BlockSpec((ssm_w + attn_w, d_model), lambda i: (0, 0)),
                  pl.BlockSpec((to, d_model), lambda i: (i, 0)),
                  pl.BlockSpec((1, d_model), lambda i: (0, 0))],
        out_specs=pl.BlockSpec((to, d_model), lambda i: (i, 0)),
        out_shape=jax.ShapeDtypeStruct((tokens, d_model), F32),
        compiler_params=_cparams("arbitrary"),
        name="out_proj",
    )(ys, oa.reshape(tokens, attn_w), za, row(g_attn), w_out.astype(BF16), x2, row(norm_post_g))
    return out.reshape(bsz, seq, d_model)


def kernel(x, norm_pre_g, w_in, b_forget, ssm_a_re, ssm_a_im, ssm_log_dt, ssm_b_re, ssm_b_im,
           ssm_c_re, ssm_c_im, ssm_d, w_glu, b_glu, g_ssm, g_attn, w_out, norm_post_g):
    h = x
    for layer in range(norm_pre_g.shape[0]):
        h = _layer(h, norm_pre_g[layer], w_in[layer], b_forget[layer], ssm_a_re[layer],
                   ssm_a_im[layer], ssm_log_dt[layer], ssm_b_re[layer], ssm_b_im[layer],
                   ssm_c_re[layer], ssm_c_im[layer], ssm_d[layer], w_glu[layer], b_glu[layer],
                   g_ssm[layer], g_attn[layer], w_out[layer], norm_post_g[layer])
    return h
```

```python
import functools
import math

import jax
import jax.numpy as jnp
from jax import lax
from jax.experimental import pallas as pl
from jax.experimental.pallas import tpu as pltpu

F32 = jnp.float32
BF16 = jnp.bfloat16

SSM_GROUP_CH = 16
SSM_STATE = 64
HEAD_DIM = 64
RMS_EPS = 1e-6

LANES = 128
SUBLANES = 8
VMEM_LIMIT_BYTES = 56 * 1024 * 1024

TM_PROJ = 512
TM_OUT = 1024
TQ_ATTN = 1024
TK_ATTN = 512
TD_ATTN = 256
QCOL_BLOCK = 256
CUM_CHUNK = 128
PV_ONES_ROWS = 2 * SUBLANES
LOG2E = math.log2(math.e)
NT_DIMS = (((1,), (1,)), ((), ()))
SG_GROUPS = LANES // SSM_GROUP_CH
SG_STATE = SG_GROUPS * SSM_STATE
NEG_BIG = -1e30


def _cparams(*sem):
    return pltpu.CompilerParams(dimension_semantics=sem, vmem_limit_bytes=VMEM_LIMIT_BYTES)


def _rms(v, g):
    return v * lax.rsqrt(jnp.mean(v * v, axis=-1, keepdims=True) + RMS_EPS) * g


def _forget_cum_kernel(ft_ref, bf_ref, aug_ref, *, seq, n_hp, chunk):
    f = ft_ref[0] + bf_ref[...]
    lf = jnp.minimum(f, 0.0) - jnp.log1p(jnp.exp(-jnp.abs(f)))
    lane = lax.broadcasted_iota(jnp.int32, lf.shape, 1)
    k = 1
    while k < seq:
        lf = lf + jnp.where(lane >= k, pltpu.roll(lf, k, axis=1), 0.0)
        k *= 2
    nb = lf * (-LOG2E)
    hi = nb.astype(BF16).astype(F32)
    mid = (nb - hi).astype(BF16).astype(F32)
    lo = (nb - hi - mid).astype(BF16).astype(F32)
    parts = (hi, mid, lo)
    rid = lax.broadcasted_iota(jnp.int32, (SUBLANES, seq), 0)
    eye = (lax.broadcasted_iota(jnp.int32, (chunk, chunk), 0)
           == lax.broadcasted_iota(jnp.int32, (chunk, chunk), 1)).astype(BF16)
    pad = jnp.zeros((LANES - SUBLANES, seq), F32)
    for p in range(n_hp):
        rows = jnp.zeros((SUBLANES, seq), F32)
        for j in range(LANES // HEAD_DIM):
            for t, part in enumerate(parts):
                src = part[2 * p + j:2 * p + j + 1, :]
                rows = jnp.where(rid == 3 * j + t, jnp.broadcast_to(src, rows.shape), rows)
        bmat = jnp.concatenate([rows, pad], axis=0).astype(BF16)
        for c in range(seq // chunk):
            blk = lax.dot_general(eye, bmat[:, c * chunk:(c + 1) * chunk], NT_DIMS,
                                  preferred_element_type=F32)
            aug_ref[0, p, c * chunk:(c + 1) * chunk, :] = blk.astype(BF16)


def _cmul(ar, ai, xr, xi):
    return ar * xr - ai * xi, ar * xi + ai * xr


def _in_proj_s5_kernel(x_ref, g_ref, w_ref, wt_ref, perm_ref, bsg_ref, csg_ref, coef_ref, d_ref,
                       wglu_ref, bglu_ref, gs_ref, ys_ref, k_ref, za_ref, qt_ref, vt_ref, ft_ref,
                       bu_ref, hb_ref, carry_ref, *, ssm_w, attn_w, n_heads, tl, tiles_per_seq):
    n_sg = ssm_w // LANES
    sg_cols = 2 * SG_STATE
    nj = tl // SUBLANES
    c0 = 2 * ssm_w

    @pl.when(pl.program_id(0) % tiles_per_seq == 0)
    def _():
        carry_ref[...] = jnp.zeros_like(carry_ref)

    xb = _rms(x_ref[...], g_ref[...]).astype(BF16)
    uz = jnp.dot(xb, w_ref[:, :c0], preferred_element_type=F32).astype(BF16)
    uzp = jnp.dot(perm_ref[0], uz, preferred_element_type=F32).astype(BF16)

    def proj_k():
        k_ref[...] = jnp.dot(xb, w_ref[:, c0:c0 + attn_w],
                             preferred_element_type=F32).astype(BF16)

    def proj_za():
        za_ref[...] = jnp.dot(xb, w_ref[:, c0 + attn_w:], preferred_element_type=F32).astype(BF16)

    def proj_qt():
        qt = lax.dot_general(wt_ref[:attn_w], xb, NT_DIMS, preferred_element_type=F32)
        qt_ref[0] = (qt * (HEAD_DIM ** -0.5 * LOG2E)).astype(BF16)

    def proj_vt():
        vt_ref[0] = lax.dot_general(wt_ref[attn_w:2 * attn_w], xb, NT_DIMS,
                                    preferred_element_type=F32).astype(BF16)
        ft = lax.dot_general(wt_ref[2 * attn_w:], xb, NT_DIMS, preferred_element_type=F32)
        ft_ref[0] = ft[:n_heads]

    fillers = [proj_k, proj_za, proj_qt, proj_vt]

    for i in range(n_sg):
        bu_ref[:, i * sg_cols:(i + 1) * sg_cols] = jnp.dot(
            uzp[:, i * LANES:(i + 1) * LANES], bsg_ref[i], preferred_element_type=F32)

    rid = lax.broadcasted_iota(jnp.int32, (SUBLANES, SG_STATE), 0)
    y_parts = []
    for i in range(n_sg):
        re = slice(i * sg_cols, i * sg_cols + SG_STATE)
        im = slice(i * sg_cols + SG_STATE, (i + 1) * sg_cols)
        cs = slice(i * SG_STATE, (i + 1) * SG_STATE)
        if fillers:
            fillers.pop(0)()
        ar, ai = coef_ref[8, :, cs], coef_ref[9, :, cs]

        hr = jnp.zeros((SUBLANES, SG_STATE), F32)
        hi = jnp.zeros((SUBLANES, SG_STATE), F32)
        for j in range(nj):
            rows = slice(j * SUBLANES, (j + 1) * SUBLANES)
            tr, ti = _cmul(ar, ai, hr, hi)
            hr, hi = tr + bu_ref[rows, re], ti + bu_ref[rows, im]
            bu_ref[rows, re] = hr
            bu_ref[rows, im] = hi

        for t in range(3):
            k = 1 << t
            tr, ti = _cmul(coef_ref[2 * t, :, cs], coef_ref[2 * t + 1, :, cs],
                           pltpu.roll(hr, k, axis=0), pltpu.roll(hi, k, axis=0))
            hr, hi = hr + tr, hi + ti
        cr = jnp.broadcast_to(carry_ref[0:1, cs], hr.shape)
        cim = jnp.broadcast_to(carry_ref[1:2, cs], hi.shape)
        tr, ti = _cmul(coef_ref[6, :, cs], coef_ref[7, :, cs], cr, cim)
        hr, hi = hr + tr, hi + ti
        carry_ref[0:1, cs] = hr[SUBLANES - 1:SUBLANES, :]
        carry_ref[1:2, cs] = hi[SUBLANES - 1:SUBLANES, :]
        gr = jnp.where(rid == 0, cr, pltpu.roll(hr, 1, axis=0))
        gi = jnp.where(rid == 0, cim, pltpu.roll(hi, 1, axis=0))

        for j in range(0, nj, 2):
            outs = []
            for jj in (j, j + 1):
                rows = slice(jj * SUBLANES, (jj + 1) * SUBLANES)
                gr, gi = _cmul(ar, ai, gr, gi)
                outs.append((bu_ref[rows, re] + gr, bu_ref[rows, im] + gi))
            rows2 = slice(j * SUBLANES, (j + 2) * SUBLANES)
            hb_ref[rows2, re] = jnp.concatenate([outs[0][0], outs[1][0]], axis=0).astype(BF16)
            hb_ref[rows2, im] = jnp.concatenate([outs[0][1], outs[1][1]], axis=0).astype(BF16)

        y_parts.append(jnp.dot(hb_ref[:, i * sg_cols:(i + 1) * sg_cols], csg_ref[i],
                               preferred_element_type=F32))

    for proj in fillers:
        proj()
    y = jnp.concatenate(y_parts, axis=-1)
    u = uzp[:, :ssm_w].astype(F32)
    z = uzp[:, ssm_w:].astype(F32)
    y = jax.nn.gelu(y + d_ref[...] * u)
    gate = jnp.dot(y.astype(BF16), wglu_ref[...], preferred_element_type=F32) + bglu_ref[...]
    y = y * jax.nn.sigmoid(gate)
    y = y * (z * jax.nn.sigmoid(z))
    ysp = _rms(y, gs_ref[...]).astype(BF16)
    ys_ref[...] = jnp.dot(perm_ref[1], ysp, preferred_element_type=F32).astype(BF16)


def _attn_kernel(qt_ref, k_ref, aug_ref, vt_ref, o_ref, qc_ref, sa_ref, sb_ref, sc_ref, sd_ref,
                 ta_ref, tb_ref, tc_ref, td_ref, m_ref, acc_ref, *, tq, tk, td):
    qi = pl.program_id(2)
    n_h = LANES // HEAD_DIM

    zero = jnp.zeros((HEAD_DIM, tq), BF16)
    for h in range(n_h):
        for hh in range(n_h):
            qc_ref[h * HEAD_DIM:(h + 1) * HEAD_DIM, hh * tq:(hh + 1) * tq] = (
                qt_ref[0, h * HEAD_DIM:(h + 1) * HEAD_DIM, :] if h == hh else zero)
    r = lax.broadcasted_iota(jnp.int32, (LANES, n_h * tq), 0)
    c = lax.broadcasted_iota(jnp.int32, (LANES, n_h * tq), 1)
    ones = (r >= 0) & (r < 3) & (c < tq)
    for h in range(1, n_h):
        ones = ones | ((r >= 3 * h) & (r < 3 * h + 3) & (c >= h * tq) & (c < (h + 1) * tq))
    qc_ref[LANES:, :] = jnp.where(ones, 1.0, 0.0).astype(BF16)

    def pv(pb, start, h):
        size = pb.shape[0]
        lhs = jnp.concatenate([vt_ref[0, h * HEAD_DIM:(h + 1) * HEAD_DIM, pl.ds(start, size)],
                               jnp.ones((PV_ONES_ROWS, size), BF16)], axis=0)
        return jnp.dot(lhs, pb, preferred_element_type=F32)

    n_pairs = qi * (tq // (2 * tk))

    def key_rows(start, size=tk):
        return jnp.concatenate([k_ref[0, pl.ds(start, size), :],
                                aug_ref[0, 0, pl.ds(start, size), :]], axis=1)

    def scores(lhs, cols, s_ref, t_ref):
        s = jnp.dot(lhs, qc_ref[:, cols], preferred_element_type=F32)
        s_ref[:, cols] = s
        t_ref[:, cols] = jnp.max(s, axis=0, keepdims=True)

    def softmax_pv(start, cols, s_ref, t_ref):
        h = cols.start // tq
        qcols = slice(cols.start - h * tq, cols.stop - h * tq)
        m_old = m_ref[:, cols]
        m_new = jnp.maximum(m_old, t_ref[:, cols])
        alpha = jnp.exp2(m_old - m_new)
        pb = jnp.exp2(s_ref[:, cols] - m_new).astype(BF16)
        m_ref[:, cols] = m_new
        acc_ref[h, :, qcols] = alpha * acc_ref[h, :, qcols] + pv(pb, start, h)

    sets = (((sa_ref, ta_ref), (sb_ref, tb_ref)), ((sc_ref, tc_ref), (sd_ref, td_ref)))
    col_blocks = [slice(c, c + QCOL_BLOCK) for c in range(0, n_h * tq, QCOL_BLOCK)]

    def pair_body(p, cur, nxt):
        base = pl.multiple_of(p * (2 * tk), 2 * tk)
        for t in range(2):
            start = base + t * tk
            if nxt is not None:
                lhs = key_rows(start + 2 * tk)
            for cols in col_blocks:
                if nxt is not None:
                    scores(lhs, cols, *nxt[t])
                softmax_pv(start, cols, *cur[t])

    n_sub = tq // td
    d0 = pl.multiple_of(qi * tq, tq)
    kk = lax.broadcasted_iota(jnp.int32, (td, td), 0)
    qq = lax.broadcasted_iota(jnp.int32, (td, td), 1)
    causal = jnp.where(kk <= qq, 0.0, NEG_BIG)
    blocks = [(h, c) for c in range(n_sub) for h in range(n_h)]
    sq = {}
    for r in range(n_sub):
        lhs = key_rows(d0 + r * td, td)
        for h, c in blocks:
            if c >= r:
                cols = slice(h * tq + c * td, h * tq + (c + 1) * td)
                s = jnp.dot(lhs, qc_ref[:, cols], preferred_element_type=F32)
                sq[h, c, r] = s + causal if r == c else s
    first_pair = [(t, cols) for t in range(2) for cols in col_blocks]
    first_lhs = [key_rows(t * tk) for t in range(2)]
    state = {}
    n_done = 0
    for r in range(n_sub):
        for h, c in blocks:
            if c < r:
                continue
            s = sq[h, c, r]
            t = jnp.max(s, axis=0, keepdims=True)
            if r == 0:
                m_new = t
            else:
                m_old, acc = state[h, c]
                m_new = jnp.maximum(m_old, t)
                alpha = jnp.exp2(m_old - m_new)
            upd = pv(jnp.exp2(s - m_new).astype(BF16), d0 + r * td, h)
            acc = upd if r == 0 else alpha * acc + upd
            state[h, c] = (m_new, acc)
            if r == c:
                m_ref[:, h * tq + c * td:h * tq + (c + 1) * td] = m_new
                acc_ref[h, :, c * td:(c + 1) * td] = acc
            n_done += 1
            if n_done % 2 == 0 and first_pair:
                t0, cols0 = first_pair.pop(0)
                scores(first_lhs[t0], cols0, *sets[0][t0])
    for t0, cols0 in first_pair:
        scores(first_lhs[t0], cols0, *sets[0][t0])

    def pair(p, carry):
        for parity in range(2):
            @pl.when(p % 2 == parity)
            def _(parity=parity):
                pair_body(p, sets[parity], sets[1 - parity])
        return carry

    lax.fori_loop(0, n_pairs - 1, pair, 0)

    for parity in range(2):
        @pl.when((n_pairs > 0) & ((n_pairs - 1) % 2 == parity))
        def _(parity=parity):
            pair_body(n_pairs - 1, sets[parity], None)

    ot = jnp.concatenate(
        [acc_ref[h, :HEAD_DIM, :] * (1.0 / acc_ref[h, HEAD_DIM:HEAD_DIM + 1, :])
         for h in range(n_h)], axis=0)
    o_ref[0] = ot.T.astype(BF16)


def _out_proj_kernel(ys_ref, oa_ref, za_ref, ga_ref, w_ref, x_ref, gp_ref, out_ref, *, ssm_w):
    z = za_ref[...].astype(F32)
    ya = _rms(oa_ref[...].astype(F32) * (z * jax.nn.sigmoid(z)), ga_ref[...]).astype(BF16)
    y = (jnp.dot(ys_ref[...], w_ref[:ssm_w, :], preferred_element_type=F32)
         + jnp.dot(ya, w_ref[ssm_w:, :], preferred_element_type=F32))
    out_ref[...] = x_ref[...] + _rms(y, gp_ref[...])


def _s5_params(a_re, a_im, log_dt, b_re, b_im, c_re, c_im, nj):
    g, n = a_re.shape
    h = b_re.shape[-1]
    n_sg = g // SG_GROUPS
    dt = jnp.exp(log_dt)[:, None]
    mag, ang = jnp.exp(dt * a_re), dt * a_im
    abar_re, abar_im = mag * jnp.cos(ang), mag * jnp.sin(ang)
    den = a_re * a_re + a_im * a_im
    nr, ni = abar_re - 1.0, abar_im
    coef_re = (nr * a_re + ni * a_im) / den
    coef_im = (ni * a_re - nr * a_im) / den
    bbar_re = coef_re[..., None] * b_re - coef_im[..., None] * b_im
    bbar_im = coef_re[..., None] * b_im + coef_im[..., None] * b_re
    eye = jnp.eye(SG_GROUPS, dtype=F32)
    bb = jnp.stack([bbar_re, bbar_im]).reshape(2, n_sg, SG_GROUPS, n, h)
    bsg = jnp.einsum('pignh,gk->ighpkn', bb, eye).reshape(n_sg, SG_GROUPS * h, 2 * SG_GROUPS * n)
    cc = jnp.stack([c_re, -c_im]).reshape(2, n_sg, SG_GROUPS, h, n)
    csg = jnp.einsum('pighn,gk->ipgnkh', cc, eye).reshape(n_sg, 2 * SG_GROUPS * n, SG_GROUPS * h)

    def apow(m):
        m = jnp.asarray(m, F32)[:, None, None]
        pm, pa = jnp.exp(m * (dt * a_re)), m * ang
        return (pm * jnp.cos(pa)).reshape(-1, g * n), (pm * jnp.sin(pa)).reshape(-1, g * n)

    rows = jnp.arange(SUBLANES)
    coef = []
    for k in (1, 2, 4):
        pr, pi = apow([nj * k])
        live = (rows >= k)[:, None]
        coef += [jnp.where(live, pr, 0.0), jnp.where(live, pi, 0.0)]
    pr, pi = apow(nj * (rows + 1))
    coef += [pr, pi]
    pr, pi = apow(jnp.ones((SUBLANES,)))
    coef += [pr, pi]
    return bsg.astype(BF16), csg.astype(BF16), jnp.stack(coef)


def _layer(x, norm_pre_g, w_in, b_forget, a_re, a_im, log_dt, b_re, b_im, c_re, c_im, d_skip,
           w_glu, b_glu, g_ssm, g_attn, w_out, norm_post_g):
    bsz, seq, d_model = x.shape
    n_heads = b_forget.shape[0]
    attn_w = n_heads * HEAD_DIM
    ssm_w = d_skip.shape[0]
    tokens = bsz * seq
    tm, to, tq = TM_PROJ, TM_OUT, TQ_ATTN
    assert seq % tm == 0 and tokens % to == 0 and seq % tq == 0
    assert ssm_w % LANES == 0 and attn_w % LANES == 0 and n_heads <= 2 * SUBLANES
    nt = seq // tm
    x2 = x.reshape(tokens, d_model)
    row = lambda v: v.reshape(1, -1).astype(F32)

    c0 = 2 * ssm_w
    w_q, w_k, w_v, w_za, w_f = (w_in[:, c0 + j * attn_w:c0 + (j + 1) * attn_w] for j in range(5))
    w_main = jnp.concatenate([w_in[:, :c0], w_k, w_za], axis=1).astype(BF16)
    w_t = jnp.concatenate([w_q.T, w_v.T, w_f.T,
                           jnp.zeros((2 * SUBLANES - n_heads, d_model), F32)], axis=0).astype(BF16)

    assert tm % (2 * SUBLANES * SUBLANES) == 0 and (tm // SUBLANES) & (tm // SUBLANES - 1) == 0
    bsg, csg, coef = _s5_params(a_re, a_im, log_dt, b_re, b_im, c_re, c_im, tm // SUBLANES)
    n_state = a_re.size
    src_row = (jnp.arange(tm) % SUBLANES) * (tm // SUBLANES) + jnp.arange(tm) // SUBLANES
    perm = (src_row[:, None] == jnp.arange(tm)[None, :])
    perms = jnp.stack([perm, perm.T]).astype(BF16)
    const2 = lambda i: (0, 0)
    const3 = lambda i: (0, 0, 0)
    time_major = lambda i: (i // nt, 0, i % nt)

    ys, k, za, qt, vt, ft = pl.pallas_call(
        functools.partial(_in_proj_s5_kernel, ssm_w=ssm_w, attn_w=attn_w, n_heads=n_heads, tl=tm,
                          tiles_per_seq=nt),
        grid=(tokens // tm,),
        in_specs=[pl.BlockSpec((tm, d_model), lambda i: (i, 0)),
                  pl.BlockSpec((1, d_model), const2),
                  pl.BlockSpec(w_main.shape, const2),
                  pl.BlockSpec(w_t.shape, const2),
                  pl.BlockSpec(perms.shape, const3),
                  pl.BlockSpec(bsg.shape, const3),
                  pl.BlockSpec(csg.shape, const3),
                  pl.BlockSpec(coef.shape, const3),
                  pl.BlockSpec((1, ssm_w), const2),
                  pl.BlockSpec((ssm_w, ssm_w), const2),
                  pl.BlockSpec((1, ssm_w), const2),
                  pl.BlockSpec((1, ssm_w), const2)],
        out_specs=[pl.BlockSpec((tm, ssm_w), lambda i: (i, 0)),
                   pl.BlockSpec((tm, attn_w), lambda i: (i, 0)),
                   pl.BlockSpec((tm, attn_w), lambda i: (i, 0)),
                   pl.BlockSpec((1, attn_w, tm), time_major),
                   pl.BlockSpec((1, attn_w, tm), time_major),
                   pl.BlockSpec((1, n_heads, tm), time_major)],
        out_shape=[jax.ShapeDtypeStruct((tokens, ssm_w), BF16),
                   jax.ShapeDtypeStruct((tokens, attn_w), BF16),
                   jax.ShapeDtypeStruct((tokens, attn_w), BF16),
                   jax.ShapeDtypeStruct((bsz, attn_w, seq), BF16),
                   jax.ShapeDtypeStruct((bsz, attn_w, seq), BF16),
                   jax.ShapeDtypeStruct((bsz, n_heads, seq), F32)],
        scratch_shapes=[pltpu.VMEM((tm, 2 * n_state), F32),
                        pltpu.VMEM((tm, 2 * n_state), BF16),
                        pltpu.VMEM((2, n_state), F32)],
        compiler_params=_cparams("arbitrary"),
        name="in_proj_s5",
    )(x2, row(norm_pre_g), w_main, w_t, perms, bsg, csg, coef, row(d_skip), w_glu.astype(BF16),
      row(b_glu), row(g_ssm))

    n_hp = n_heads // (LANES // HEAD_DIM)
    aug = pl.pallas_call(
        functools.partial(_forget_cum_kernel, seq=seq, n_hp=n_hp, chunk=CUM_CHUNK),
        grid=(bsz,),
        in_specs=[pl.BlockSpec((1, n_heads, seq), lambda b: (b, 0, 0)),
                  pl.BlockSpec((n_heads, 1), lambda b: (0, 0))],
        out_specs=pl.BlockSpec((1, n_hp, seq, LANES), lambda b: (b, 0, 0, 0)),
        out_shape=jax.ShapeDtypeStruct((bsz, n_hp, seq, LANES), BF16),
        compiler_params=_cparams("arbitrary"),
        name="forget_cum",
    )(ft, b_forget.reshape(n_heads, 1).astype(F32))

    tk, td = TK_ATTN, TD_ATTN
    assert tq % (2 * tk) == 0 and tq % td == 0 and 3 * (LANES // HEAD_DIM) <= LANES
    nq = seq // tq
    wq = (LANES // HEAD_DIM) * tq
    oa = pl.pallas_call(
        functools.partial(_attn_kernel, tq=tq, tk=tk, td=td),
        grid=(bsz, n_hp, nq),
        in_specs=[pl.BlockSpec((1, LANES, tq), lambda b, p, i: (b, p, i)),
                  pl.BlockSpec((1, seq, LANES), lambda b, p, i: (b, 0, p)),
                  pl.BlockSpec((1, 1, seq, LANES), lambda b, p, i: (b, p, 0, 0)),
                  pl.BlockSpec((1, LANES, seq), lambda b, p, i: (b, p, 0))],
        out_specs=pl.BlockSpec((1, tq, LANES), lambda b, p, i: (b, i, p)),
        out_shape=jax.ShapeDtypeStruct((bsz, seq, attn_w), BF16),
        scratch_shapes=[pltpu.VMEM((2 * LANES, wq), BF16),
                        pltpu.VMEM((tk, wq), F32),
                        pltpu.VMEM((tk, wq), F32),
                        pltpu.VMEM((tk, wq), F32),
                        pltpu.VMEM((tk, wq), F32),
                        pltpu.VMEM((1, wq), F32),
                        pltpu.VMEM((1, wq), F32),
                        pltpu.VMEM((1, wq), F32),
                        pltpu.VMEM((1, wq), F32),
                        pltpu.VMEM((1, wq), F32),
                        pltpu.VMEM((LANES // HEAD_DIM, HEAD_DIM + PV_ONES_ROWS, tq), F32)],
        compiler_params=_cparams("arbitrary", "arbitrary", "arbitrary"),
        name="fox_attn",
    )(qt, k.reshape(bsz, seq, attn_w), aug, vt)

    out = pl.pallas_call(
        functools.partial(_out_proj_kernel, ssm_w=ssm_w),
        grid=(tokens // to,),
        in_specs=[pl.BlockSpec((to, ssm_w), lambda i: (i, 0)),
                  pl.BlockSpec((to, attn_w), lambda i: (i, 0)),
                  pl.BlockSpec((to, attn_w), lambda i: (i, 0)),
                  pl.BlockSpec((1, attn_w), lambda i: (0, 0)),
                  pl.BlockSpec((ssm_w + attn_w, d_model), lambda i: (0, 0)),
                  pl.BlockSpec((to, d_model), lambda i: (i, 0)),
                  pl.BlockSpec((1, d_model), lambda i: (0, 0))],
        out_specs=pl.BlockSpec((to, d_model), lambda i: (i, 0)),
        out_shape=jax.ShapeDtypeStruct((tokens, d_model), F32),
        compiler_params=_cparams("arbitrary"),
        name="out_proj",
    )(ys, oa.reshape(tokens, attn_w), za, row(g_attn), w_out.astype(BF16), x2, row(norm_post_g))
    return out.reshape(bsz, seq, d_model)


def kernel(x, norm_pre_g, w_in, b_forget, ssm_a_re, ssm_a_im, ssm_log_dt, ssm_b_re, ssm_b_im,
           ssm_c_re, ssm_c_im, ssm_d, w_glu, b_glu, g_ssm, g_attn, w_out, norm_post_g):
    h = x
    for layer in range(norm_pre_g.shape[0]):
        h = _layer(h, norm_pre_g[layer], w_in[layer], b_forget[layer], ssm_a_re[layer],
                   ssm_a_im[layer], ssm_log_dt[layer], ssm_b_re[layer], ssm_b_im[layer],
                   ssm_c_re[layer], ssm_c_im[layer], ssm_d[layer], w_glu[layer], b_glu[layer],
                   g_ssm[layer], g_attn[layer], w_out[layer], norm_post_g[layer])
    return h
```
